```python
import jax, jax.numpy as jnp
from jax import lax
import numpy as np

D_MODEL = 1024
BATCH = 2
SEQ = 8192
DEPTH = 1

CHUNK = 64
Q_BLOCK = 128
HEAD_DIM = 64
FOX_HEADS = 8
SB_HEADS = 8
FOX_WIDTH = FOX_HEADS * HEAD_DIM
SB_WIDTH = SB_HEADS * HEAD_DIM
MIX_WIDTH = FOX_WIDTH + SB_WIDTH
IN_SPLITS = (FOX_WIDTH, FOX_WIDTH, FOX_WIDTH, FOX_WIDTH, FOX_HEADS,
             SB_WIDTH, SB_WIDTH, SB_WIDTH, SB_WIDTH)
IN_WIDTH = 4 * FOX_WIDTH + FOX_HEADS + 4 * SB_WIDTH
DEEPNORM_ALPHA = (2.0 * DEPTH) ** 0.25
DEEPNORM_BETA = (8.0 * DEPTH) ** -0.25
LN_EPS = 1e-5

kernel_name = "hybrid_fox_stickbreaking_deepnorm_adaln"


def _layer_norm(x, gain=None, bias=None):
    xf = x.astype(jnp.float32)
    mu = jnp.mean(xf, axis=-1, keepdims=True)
    var = jnp.mean(jnp.square(xf - mu), axis=-1, keepdims=True)
    y = (xf - mu) * lax.rsqrt(var + LN_EPS)
    if gain is not None:
        y = y * gain.astype(jnp.float32) + bias.astype(jnp.float32)
    return y


def _split_columns(h):
    outs, off = [], 0
    for w in IN_SPLITS:
        outs.append(h[..., off:off + w])
        off += w
    return outs


def _to_heads(t, n_heads):
    b, s, _ = t.shape
    return t.reshape(b, s, n_heads, HEAD_DIM).transpose(0, 2, 1, 3)


def _from_heads(t):
    b, h, s, d = t.shape
    return t.transpose(0, 2, 1, 3).reshape(b, s, h * d)


def _fox_block(q, k, v, f_q, f_k, q_pos, k_pos):
    scale = HEAD_DIM ** -0.5
    s = jnp.einsum('bhqd,bhkd->bhqk', q, k).astype(jnp.float32) * scale
    s = s + (f_q[..., :, None] - f_k[..., None, :])
    causal = k_pos[None, :] <= q_pos[:, None]
    s = jnp.where(causal, s, -jnp.inf)
    p = jax.nn.softmax(s, axis=-1)
    return jnp.einsum('bhqk,bhkd->bhqd', p.astype(v.dtype), v)


def _stick_breaking_block(q, k, v, q_pos, k_pos):
    scale = HEAD_DIM ** -0.5
    z = jnp.einsum('bhqd,bhkd->bhqk', q, k).astype(jnp.float32) * scale
    strict = k_pos[None, :] < q_pos[:, None]
    log_beta = jax.nn.log_sigmoid(z)
    log_keep = jnp.where(strict, jax.nn.log_sigmoid(-z), 0.0)
    later = lax.cumsum(log_keep, axis=3, reverse=True) - log_keep
    w = jnp.where(strict, jnp.exp(log_beta + later), 0.0)
    return jnp.einsum('bhqk,bhkd->bhqd', w.astype(v.dtype), v)


def setup_inputs(seed: int = 0) -> dict:
    key = jax.random.key(seed)
    ks = jax.random.split(key, 16)
    x = jax.random.normal(ks[0], (BATCH, SEQ, D_MODEL), jnp.float32)
    c = jax.random.normal(ks[1], (BATCH, D_MODEL), jnp.float32)
    w_ada = jax.random.normal(ks[2], (DEPTH, D_MODEL, 3 * D_MODEL), jnp.float32) * (0.5 * D_MODEL ** -0.5)
    b_ada = 0.02 * jax.random.normal(ks[3], (DEPTH, 3 * D_MODEL), jnp.float32)
    col_scale = []
    for idx, w in enumerate(IN_SPLITS):
        is_value = idx in (2, 6)
        col_scale.append(jnp.full((w,), DEEPNORM_BETA if is_value else 1.0, jnp.float32))
    col_scale = jnp.concatenate(col_scale)
    w_in = jax.random.normal(ks[4], (DEPTH, D_MODEL, IN_WIDTH), jnp.float32) * (D_MODEL ** -0.5) * col_scale
    b_f = 2.0 + 0.5 * jax.random.normal(ks[5], (DEPTH, FOX_HEADS), jnp.float32)
    w_out = jax.random.normal(ks[6], (DEPTH, MIX_WIDTH, D_MODEL), jnp.float32) * (MIX_WIDTH ** -0.5) * DEEPNORM_BETA
    ln_g = 1.0 + 0.02 * jax.random.normal(ks[7], (DEPTH, D_MODEL), jnp.float32)
    ln_b = 0.02 * jax.random.normal(ks[8], (DEPTH, D_MODEL), jnp.float32)
    return {"x": x, "c": c, "w_ada": w_ada, "b_ada": b_ada, "w_in": w_in, "b_f": b_f,
            "w_out": w_out, "ln_g": ln_g, "ln_b": ln_b}


def reference(x, c, w_ada, b_ada, w_in, b_f, w_out, ln_g, ln_b):
    dtype = x.dtype
    seq = x.shape[1]
    n_blocks = seq // Q_BLOCK
    pos = jnp.arange(seq, dtype=jnp.int32)
    for layer in range(DEPTH):
        mod = jax.nn.silu(c) @ w_ada[layer] + b_ada[layer]
        shift = mod[:, :D_MODEL]
        scale = mod[:, D_MODEL:2 * D_MODEL]
        gate = mod[:, 2 * D_MODEL:]
        u = (_layer_norm(x) * (1.0 + scale[:, None, :].astype(jnp.float32))
             + shift[:, None, :].astype(jnp.float32)).astype(dtype)

        h = u @ w_in[layer]
        fq, fk, fv, fg, ff, sq, sk, sv, sg = _split_columns(h)

        log_f = jax.nn.log_sigmoid(ff.astype(jnp.float32) + b_f[layer].astype(jnp.float32))
        f_cum = jnp.cumsum(log_f, axis=1).transpose(0, 2, 1)

        fq, fk, fv = _to_heads(fq, FOX_HEADS), _to_heads(fk, FOX_HEADS), _to_heads(fv, FOX_HEADS)
        sq, sk, sv = _to_heads(sq, SB_HEADS), _to_heads(sk, SB_HEADS), _to_heads(sv, SB_HEADS)

        fox_out, sb_out = [], []
        for i in range(n_blocks):
            qs, ke = i * Q_BLOCK, (i + 1) * Q_BLOCK
            q_pos, k_pos = pos[qs:ke], pos[:ke]
            fox_out.append(_fox_block(fq[:, :, qs:ke], fk[:, :, :ke], fv[:, :, :ke],
                                      f_cum[:, :, qs:ke], f_cum[:, :, :ke], q_pos, k_pos))
            sb_out.append(_stick_breaking_block(sq[:, :, qs:ke], sk[:, :, :ke], sv[:, :, :ke],
                                                q_pos, k_pos))
        y_fox = _from_heads(jnp.concatenate(fox_out, axis=2)) * jax.nn.silu(fg)
        y_sb = _from_heads(jnp.concatenate(sb_out, axis=2)) * jax.nn.silu(sg)
        y = jnp.concatenate([y_fox, y_sb], axis=-1) @ w_out[layer]

        resid = DEEPNORM_ALPHA * x.astype(jnp.float32) + gate[:, None, :].astype(jnp.float32) * y.astype(jnp.float32)
        x = _layer_norm(resid, ln_g[layer], ln_b[layer]).astype(dtype)
    return x
```

```python
import functools

import jax
import jax.numpy as jnp
from jax import lax
from jax.experimental import pallas as pl
from jax.experimental.pallas import tpu as pltpu

D_MODEL = 1024
HEAD_DIM = 64
N_HEADS = 8
WIDTH = N_HEADS * HEAD_DIM
LN_EPS = 1e-5
DEPTH = 1
DEEPNORM_ALPHA = (2.0 * DEPTH) ** 0.25

TS = 512
BQ = 256
BK = 256
STRANDS = 8
STRAND_LEN = BK // STRANDS
KPAD = 128
VROWS = 80

F32 = jnp.float32
BF16 = jnp.bfloat16
NT_DIMS = (((1,), (1,)), ((), ()))


def _split3(f):
    hi = f.astype(BF16)
    r1 = f - hi.astype(F32)
    mid = r1.astype(BF16)
    lo = (r1 - mid.astype(F32)).astype(BF16)
    return hi, mid, lo


def _ada_kernel(c_ref, w_ref, b_ref, o_ref):
    c = c_ref[...]
    a = c * jax.nn.sigmoid(c)
    o_ref[...] = jnp.dot(a, w_ref[...], preferred_element_type=F32,
                         precision=lax.Precision.HIGHEST) + b_ref[...]


def _ada(c8, w_ada, b_ada):
    n = w_ada.shape[1]
    return pl.pallas_call(
        _ada_kernel,
        out_shape=jax.ShapeDtypeStruct((8, n), F32),
        grid=(n // D_MODEL,),
        in_specs=[pl.BlockSpec((8, D_MODEL), lambda j: (0, 0)),
                  pl.BlockSpec((D_MODEL, D_MODEL), lambda j: (0, j)),
                  pl.BlockSpec((1, D_MODEL), lambda j: (0, j))],
        out_specs=pl.BlockSpec((8, D_MODEL), lambda j: (0, j)),
        name="ada",
    )(c8, w_ada, b_ada)


def _proj_kernel(x_ref, mod_ref, wfq_ref, wfk_ref, wfv_ref, wfg_ref, wff_ref,
                 wsq_ref, wsk_ref, wsv_ref, wsg_ref, bf_ref, tri_ref, perm_ref,
                 qf_ref, kf_ref, vf_ref, gf_ref, qs_ref, ks_ref, vs_ref, gs_ref,
                 carry_ref):
    @pl.when(pl.program_id(1) == 0)
    def _():
        carry_ref[...] = jnp.zeros_like(carry_ref)

    x = x_ref[0]
    mu = jnp.mean(x, axis=-1, keepdims=True)
    xc = x - mu
    var = jnp.mean(xc * xc, axis=-1, keepdims=True)
    shift = mod_ref[0, 0:1, :]
    scale = mod_ref[0, 1:2, :]
    u = xc * lax.rsqrt(var + LN_EPS) * (1.0 + scale) + shift
    ub = u.astype(BF16)
    up = jnp.dot(perm_ref[...], ub, preferred_element_type=F32).astype(BF16)

    def proj(w_ref, lhs):
        return lax.dot_general(w_ref[...], lhs, NT_DIMS, preferred_element_type=F32)

    ff = proj(wff_ref, ub)[0:N_HEADS]
    logf = jax.nn.log_sigmoid(ff + bf_ref[...])
    local = jnp.zeros((N_HEADS, TS), F32)
    for part in _split3(logf):
        local = local + jnp.dot(part, tri_ref[...], preferred_element_type=F32)
    fcum = local + carry_ref[:, 0:1]
    carry_ref[...] = jnp.broadcast_to(fcum[:, TS - 1:TS], carry_ref.shape)
    nhi, nmid, nlo = _split3(-fcum)

    zeros_pad = jnp.zeros((KPAD - HEAD_DIM - 16, TS), BF16)
    ones3 = (lax.broadcasted_iota(jnp.int32, (16, TS), 0) < 3).astype(BF16)
    ones1 = (lax.broadcasted_iota(jnp.int32, (16, TS), 0) < 1).astype(BF16)

    qf = (proj(wfq_ref, ub) * (HEAD_DIM ** -0.5)).astype(BF16)
    kf = proj(wfk_ref, ub).astype(BF16)
    vf = proj(wfv_ref, ub).astype(BF16)
    g = proj(wfg_ref, ub)
    gf_ref[0] = g * jax.nn.sigmoid(g)
    for h in range(N_HEADS):
        rows = slice(h * HEAD_DIM, (h + 1) * HEAD_DIM)
        qf_ref[0, h, 0:HEAD_DIM, :] = qf[rows]
        qf_ref[0, h, HEAD_DIM:HEAD_DIM + 16, :] = ones3
        qf_ref[0, h, HEAD_DIM + 16:KPAD, :] = zeros_pad
        kf_ref[0, h, 0:HEAD_DIM, :] = kf[rows]
        aug = jnp.concatenate(
            [nhi[h:h + 1], nmid[h:h + 1], nlo[h:h + 1], jnp.zeros((13, TS), BF16)], axis=0)
        kf_ref[0, h, HEAD_DIM:HEAD_DIM + 16, :] = aug
        kf_ref[0, h, HEAD_DIM + 16:KPAD, :] = zeros_pad
        vf_ref[0, h, 0:HEAD_DIM, :] = vf[rows]
        vf_ref[0, h, HEAD_DIM:VROWS, :] = ones1

    qs = (proj(wsq_ref, ub) * (HEAD_DIM ** -0.5)).astype(BF16)
    ks = proj(wsk_ref, up).astype(BF16)
    vs = proj(wsv_ref, up).astype(BF16)
    g = proj(wsg_ref, ub)
    gs_ref[0] = g * jax.nn.sigmoid(g)
    zeros_half = jnp.zeros((KPAD - HEAD_DIM, TS), BF16)
    for h in range(N_HEADS):
        rows = slice(h * HEAD_DIM, (h + 1) * HEAD_DIM)
        qs_ref[0, h, 0:HEAD_DIM, :] = qs[rows]
        qs_ref[0, h, HEAD_DIM:KPAD, :] = zeros_half
        ks_ref[0, h, 0:HEAD_DIM, :] = ks[rows]
        ks_ref[0, h, HEAD_DIM:KPAD, :] = zeros_half
        vs_ref[0, h] = vs[rows]


def _proj(x, mod3, weights, b_f, tri, perm):
    B, S, D = x.shape
    full = lambda a: pl.BlockSpec(a.shape, lambda b, s: (0,) * a.ndim)
    head_spec = lambda r: pl.BlockSpec((1, N_HEADS, r, TS), lambda b, s: (b, 0, 0, s))
    flat_spec = pl.BlockSpec((1, WIDTH, TS), lambda b, s: (b, 0, s))
    head_shape = lambda r: jax.ShapeDtypeStruct((B, N_HEADS, r, S), BF16)
    flat_shape = jax.ShapeDtypeStruct((B, WIDTH, S), F32)
    return pl.pallas_call(
        _proj_kernel,
        out_shape=(head_shape(KPAD), head_shape(KPAD), head_shape(VROWS), flat_shape,
                   head_shape(KPAD), head_shape(KPAD), head_shape(HEAD_DIM), flat_shape),
        grid=(B, S // TS),
        in_specs=[pl.BlockSpec((1, TS, D), lambda b, s: (b, s, 0)),
                  pl.BlockSpec((1, 3, D), lambda b, s: (b, 0, 0))]
                 + [full(w) for w in weights] + [full(b_f), full(tri), full(perm)],
        out_specs=(head_spec(KPAD), head_spec(KPAD), head_spec(VROWS), flat_spec,
                   head_spec(KPAD), head_spec(KPAD), head_spec(HEAD_DIM), flat_spec),
        scratch_shapes=[pltpu.VMEM((N_HEADS, 128), F32)],
        compiler_params=pltpu.CompilerParams(
            dimension_semantics=("arbitrary", "arbitrary"),
            vmem_limit_bytes=56 * 1024 * 1024),
        name="proj",
    )(x, mod3, *weights, b_f, tri, perm)


def _transpose_keys(k_ref, krows_ref, seq):
    def body(c, carry):
        start = pl.multiple_of(c * BK, BK)
        blk = k_ref[0, 0, :, pl.ds(start, BK)].astype(F32)
        krows_ref[pl.ds(start, BK), :] = blk.T.astype(BF16)
        return carry
    lax.fori_loop(0, seq // BK, body, 0)


def _fox_kernel(q_ref, k_ref, v_ref, o_ref, krows_ref, *, seq):
    _transpose_keys(k_ref, krows_ref, seq)
    kpos = lax.broadcasted_iota(jnp.int32, (BK, BQ), 0)
    qpos = lax.broadcasted_iota(jnp.int32, (BK, BQ), 1)
    causal = kpos <= qpos

    def qbody(qi, carry):
        qs = pl.multiple_of(qi * BQ, BQ)
        qt = q_ref[0, 0, :, pl.ds(qs, BQ)]

        s = jnp.dot(krows_ref[pl.ds(qs, BK), :], qt, preferred_element_type=F32)
        s = jnp.where(causal, s, -jnp.inf)
        m = jnp.max(s, axis=0, keepdims=True)
        p = jnp.exp(s - m).astype(BF16)
        acc = jnp.dot(v_ref[0, 0, :, pl.ds(qs, BK)], p, preferred_element_type=F32)

        def kbody(kj, mc):
            m, acc = mc
            ks = pl.multiple_of(kj * BK, BK)
            s = jnp.dot(krows_ref[pl.ds(ks, BK), :], qt, preferred_element_type=F32)
            m_new = jnp.maximum(m, jnp.max(s, axis=0, keepdims=True))
            alpha = jnp.exp(m - m_new)
            p = jnp.exp(s - m_new).astype(BF16)
            pv = jnp.dot(v_ref[0, 0, :, pl.ds(ks, BK)], p, preferred_element_type=F32)
            return m_new, alpha * acc + pv

        m, acc = lax.fori_loop(0, qi, kbody, (m, acc))
        o_ref[0, :, pl.ds(qs, BQ)] = acc[0:HEAD_DIM] / acc[HEAD_DIM:HEAD_DIM + 1]
        return carry

    lax.fori_loop(0, seq // BQ, qbody, 0)


def _fox(qf, kf, vf):
    B, H, _, S = qf.shape
    spec = lambda r: pl.BlockSpec((1, 1, r, S), lambda b, h: (b, h, 0, 0))
    return pl.pallas_call(
        functools.partial(_fox_kernel, seq=S),
        out_shape=jax.ShapeDtypeStruct((B, H * HEAD_DIM, S), F32),
        grid=(B, H),
        in_specs=[spec(KPAD), spec(KPAD), spec(VROWS)],
        out_specs=pl.BlockSpec((1, HEAD_DIM, S), lambda b, h: (b, h, 0)),
        scratch_shapes=[pltpu.VMEM((S, KPAD), BF16)],
        compiler_params=pltpu.CompilerParams(
            dimension_semantics=("arbitrary", "arbitrary"),
            vmem_limit_bytes=48 * 1024 * 1024),
        name="fox",
    )(qf, kf, vf)


def _suffix_excl_sublanes(t):
    r = lax.broadcasted_iota(jnp.int32, t.shape, 0)
    a = t
    for d in (1, 2, 4):
        a = a + jnp.where(r + d < STRANDS, pltpu.roll(a, STRANDS - d, axis=0), 0.0)
    return a - t


def _sb_tile(z, carry, strict):
    sp = jnp.log(1.0 + jnp.exp(-jnp.abs(z)))
    mn = jnp.minimum(z, 0.0)
    lb = mn - sp
    lk = (mn - z) - sp
    if strict is not None:
        lk = jnp.where(strict, lk, 0.0)
    lb3 = lb.reshape(STRAND_LEN, STRANDS, BQ)
    lk3 = lk.reshape(STRAND_LEN, STRANDS, BQ)
    total = lk3[0]
    for i in range(1, STRAND_LEN):
        total = total + lk3[i]
    run = carry + _suffix_excl_sublanes(total)
    block_sum = run[0:1] + total[0:1]
    ws = [None] * STRAND_LEN
    for i in reversed(range(STRAND_LEN)):
        ws[i] = jnp.exp(lb3[i] + run)
        run = run + lk3[i]
    w = jnp.concatenate(ws, axis=0)
    if strict is not None:
        w = jnp.where(strict, w, 0.0)
    return w.astype(BF16), jnp.broadcast_to(block_sum, carry.shape)


def _sb_kernel(q_ref, k_ref, v_ref, o_ref, krows_ref, *, seq):
    _transpose_keys(k_ref, krows_ref, seq)
    row = lax.broadcasted_iota(jnp.int32, (BK, BQ), 0)
    kpos = (row % STRANDS) * STRAND_LEN + row // STRANDS
    qpos = lax.broadcasted_iota(jnp.int32, (BK, BQ), 1)
    strict = kpos < qpos

    def qbody(qi, carry):
        qs = pl.multiple_of(qi * BQ, BQ)
        qt = q_ref[0, 0, :, pl.ds(qs, BQ)]

        z = jnp.dot(krows_ref[pl.ds(qs, BK), :], qt, preferred_element_type=F32)
        w, c = _sb_tile(z, jnp.zeros((STRANDS, BQ), F32), strict)
        acc = jnp.dot(v_ref[0, 0, :, pl.ds(qs, BK)], w, preferred_element_type=F32)

        def kbody(t, cc):
            c, acc = cc
            ks = pl.multiple_of((qi - 1 - t) * BK, BK)
            z = jnp.dot(krows_ref[pl.ds(ks, BK), :], qt, preferred_element_type=F32)
            w, c = _sb_tile(z, c, None)
            acc = acc + jnp.dot(v_ref[0, 0, :, pl.ds(ks, BK)], w, preferred_element_type=F32)
            return c, acc

        c, acc = lax.fori_loop(0, qi, kbody, (c, acc))
        o_ref[0, :, pl.ds(qs, BQ)] = acc
        return carry

    lax.fori_loop(0, seq // BQ, qbody, 0)


def _sb(qs, ks, vs):
    B, H, _, S = qs.shape
    spec = lambda r: pl.BlockSpec((1, 1, r, S), lambda b, h: (b, h, 0, 0))
    return pl.pallas_call(
        functools.partial(_sb_kernel, seq=S),
        out_shape=jax.ShapeDtypeStruct((B, H * HEAD_DIM, S), F32),
        grid=(B, H),
        in_specs=[spec(KPAD), spec(KPAD), spec(HEAD_DIM)],
        out_specs=pl.BlockSpec((1, HEAD_DIM, S), lambda b, h: (b, h, 0)),
        scratch_shapes=[pltpu.VMEM((S, KPAD), BF16)],
        compiler_params=pltpu.CompilerParams(
            dimension_semantics=("arbitrary", "arbitrary"),
            vmem_limit_bytes=48 * 1024 * 1024),
        name="sb",
    )(qs, ks, vs)


def _out_kernel(yf_ref, ys_ref, gf_ref, gs_ref, x_ref, mod_ref, w_ref, lng_ref, lnb_ref, o_ref):
    yg = jnp.concatenate([yf_ref[0] * gf_ref[0], ys_ref[0] * gs_ref[0]], axis=0)
    y = jnp.dot(yg.T.astype(BF16), w_ref[...], preferred_element_type=F32)
    gate = mod_ref[0, 2:3, :]
    resid = DEEPNORM_ALPHA * x_ref[0] + gate * y
    mu = jnp.mean(resid, axis=-1, keepdims=True)
    rc = resid - mu
    var = jnp.mean(rc * rc, axis=-1, keepdims=True)
    o_ref[0] = rc * lax.rsqrt(var + LN_EPS) * lng_ref[...] + lnb_ref[...]


def _out(yf, ys, gf, gs, x, mod3, w_out, ln_g, ln_b):
    B, S, D = x.shape
    flat_spec = pl.BlockSpec((1, WIDTH, TS), lambda b, s: (b, 0, s))
    full = lambda a: pl.BlockSpec(a.shape, lambda b, s: (0,) * a.ndim)
    return pl.pallas_call(
        _out_kernel,
        out_shape=jax.ShapeDtypeStruct((B, S, D), F32),
        grid=(B, S // TS),
        in_specs=[flat_spec, flat_spec, flat_spec, flat_spec,
                  pl.BlockSpec((1, TS, D), lambda b, s: (b, s, 0)),
                  pl.BlockSpec((1, 3, D), lambda b, s: (b, 0, 0)),
                  full(w_out), full(ln_g), full(ln_b)],
        out_specs=pl.BlockSpec((1, TS, D), lambda b, s: (b, s, 0)),
        compiler_params=pltpu.CompilerParams(
            dimension_semantics=("arbitrary", "arbitrary"),
            vmem_limit_bytes=48 * 1024 * 1024),
        name="out",
    )(yf, ys, gf, gs, x, mod3, w_out, ln_g, ln_b)


def kernel(x, c, w_ada, b_ada, w_in, b_f, w_out, ln_g, ln_b):
    B, S, D = x.shape
    assert D == D_MODEL and S % TS == 0 and w_ada.shape[0] == DEPTH
    tri = (jnp.arange(TS)[:, None] <= jnp.arange(TS)[None, :]).astype(BF16)
    p = jnp.arange(TS)
    src = (p // BK) * BK + (p % STRANDS) * STRAND_LEN + (p % BK) // STRANDS
    perm = (src[:, None] == jnp.arange(TS)[None, :]).astype(BF16)
    for layer in range(DEPTH):
        c8 = jnp.pad(c, ((0, 8 - B), (0, 0)))
        mod = _ada(c8, w_ada[layer], b_ada[layer][None, :])[:B]
        mod3 = mod.reshape(B, 3, D)
        wt = w_in[layer].T.astype(BF16)
        o = 0
        parts = []
        for wdt in (WIDTH, WIDTH, WIDTH, WIDTH, N_HEADS, WIDTH, WIDTH, WIDTH, WIDTH):
            parts.append(wt[o:o + wdt])
            o += wdt
        wfq, wfk, wfv, wfg, wff, wsq, wsk, wsv, wsg = parts
        wff = jnp.pad(wff, ((0, 16 - N_HEADS), (0, 0)))
        weights = (wfq, wfk, wfv, wfg, wff, wsq, wsk, wsv, wsg)
        qf, kf, vf, gf, qs, ks, vs, gs = _proj(x, mod3, weights, b_f[layer][:, None], tri, perm)
        yf = _fox(qf, kf, vf)
        ys = _sb(qs, ks, vs)
        x = _out(yf, ys, gf, gs, x, mod3, w_out[layer].astype(BF16),
                 ln_g[layer][None, :], ln_b[layer][None, :])
    return x
```

```python
import functools

import jax
import jax.numpy as jnp
from jax import lax
from jax.experimental import pallas as pl
from jax.experimental.pallas import tpu as pltpu

D_MODEL = 1024
HEAD_DIM = 64
N_HEADS = 8
WIDTH = N_HEADS * HEAD_DIM
LN_EPS = 1e-5
DEPTH = 1
DEEPNORM_ALPHA = (2.0 * DEPTH) ** 0.25

TS = 512
BQ = 512
BK = 512
HB = 4
SUB = 256
STRANDS = 8
STRAND_LEN = SUB // STRANDS
KPAD = 128
VROWS = 80
LOG2E = 1.4426950408889634

F32 = jnp.float32
BF16 = jnp.bfloat16
NT_DIMS = (((1,), (1,)), ((), ()))


def _split3(f):
    hi = f.astype(BF16)
    r1 = f - hi.astype(F32)
    mid = r1.astype(BF16)
    lo = (r1 - mid.astype(F32)).astype(BF16)
    return hi, mid, lo


def _ada_kernel(c_ref, w_ref, b_ref, o_ref):
    c = c_ref[...]
    a = c * jax.nn.sigmoid(c)
    o_ref[...] = jnp.dot(a, w_ref[...], preferred_element_type=F32,
                         precision=lax.Precision.HIGHEST) + b_ref[...]


def _ada(c8, w_ada, b_ada):
    n = w_ada.shape[1]
    return pl.pallas_call(
        _ada_kernel,
        out_shape=jax.ShapeDtypeStruct((8, n), F32),
        grid=(n // D_MODEL,),
        in_specs=[pl.BlockSpec((8, D_MODEL), lambda j: (0, 0)),
                  pl.BlockSpec((D_MODEL, D_MODEL), lambda j: (0, j)),
                  pl.BlockSpec((1, D_MODEL), lambda j: (0, j))],
        out_specs=pl.BlockSpec((8, D_MODEL), lambda j: (0, j)),
        name="ada",
    )(c8, w_ada, b_ada)


def _proj_kernel(x_ref, mod_ref, wfq_ref, wfk_ref, wfv_ref, wfg_ref, wff_ref,
                 wsq_ref, wsk_ref, wsv_ref, wsg_ref, bf_ref, tri_ref, perm_ref,
                 qf_ref, kf_ref, vf_ref, gf_ref, qs_ref, ks_ref, vs_ref, gs_ref,
                 carry_ref):
    @pl.when(pl.program_id(1) == 0)
    def _():
        carry_ref[...] = jnp.zeros_like(carry_ref)

    x = x_ref[0]
    mu = jnp.mean(x, axis=-1, keepdims=True)
    xc = x - mu
    var = jnp.mean(xc * xc, axis=-1, keepdims=True)
    shift = mod_ref[0, 0:1, :]
    scale = mod_ref[0, 1:2, :]
    u = xc * lax.rsqrt(var + LN_EPS) * (1.0 + scale) + shift
    ub = u.astype(BF16)
    up = jnp.dot(perm_ref[...], ub, preferred_element_type=F32).astype(BF16)

    def proj(w_ref, lhs):
        return lax.dot_general(w_ref[...], lhs, NT_DIMS, preferred_element_type=F32)

    ff = proj(wff_ref, ub)[0:N_HEADS]
    logf = jax.nn.log_sigmoid(ff + bf_ref[...])
    local = jnp.zeros((N_HEADS, TS), F32)
    for part in _split3(logf):
        local = local + jnp.dot(part, tri_ref[...], preferred_element_type=F32)
    fcum = local + carry_ref[:, 0:1]
    carry_ref[...] = jnp.broadcast_to(fcum[:, TS - 1:TS], carry_ref.shape)
    nsplit = [t.astype(F32) for t in _split3(-LOG2E * fcum)]

    zeros_pad = jnp.zeros((KPAD - HEAD_DIM - 16, TS), BF16)
    zeros_half = jnp.zeros((KPAD - HEAD_DIM, TS), BF16)
    ones3 = (lax.broadcasted_iota(jnp.int32, (16, TS), 0) < 3).astype(BF16)
    ones1 = (lax.broadcasted_iota(jnp.int32, (16, TS), 0) < 1).astype(BF16)

    qf = (proj(wfq_ref, ub) * (LOG2E * HEAD_DIM ** -0.5)).astype(BF16)
    kf = proj(wfk_ref, ub)
    vf = proj(wfv_ref, ub).astype(BF16)
    g = proj(wfg_ref, ub)
    gf_ref[0] = g * jax.nn.sigmoid(g)
    for h in range(N_HEADS):
        rows = slice(h * HEAD_DIM, (h + 1) * HEAD_DIM)
        qf_ref[0, h, 0:HEAD_DIM, :] = qf[rows]
        qf_ref[0, h, HEAD_DIM:HEAD_DIM + 16, :] = ones3
        qf_ref[0, h, HEAD_DIM + 16:KPAD, :] = zeros_pad
        kaug = jnp.concatenate(
            [kf[rows]] + [t[h:h + 1] for t in nsplit]
            + [jnp.zeros((KPAD - HEAD_DIM - 3, TS), F32)], axis=0)
        kf_ref[0, h] = kaug.T.astype(BF16)
        vf_ref[0, h, 0:HEAD_DIM, :] = vf[rows]
        vf_ref[0, h, HEAD_DIM:VROWS, :] = ones1

    qs = (proj(wsq_ref, ub) * (LOG2E * HEAD_DIM ** -0.5)).astype(BF16)
    ks = proj(wsk_ref, up)
    vs = proj(wsv_ref, up).astype(BF16)
    g = proj(wsg_ref, ub)
    gs_ref[0] = g * jax.nn.sigmoid(g)
    for h in range(N_HEADS):
        rows = slice(h * HEAD_DIM, (h + 1) * HEAD_DIM)
        qs_ref[0, h, 0:HEAD_DIM, :] = qs[rows]
        qs_ref[0, h, HEAD_DIM:KPAD, :] = zeros_half
        kpad = jnp.concatenate([ks[rows], jnp.zeros((KPAD - HEAD_DIM, TS), F32)], axis=0)
        ks_ref[0, h] = kpad.T.astype(BF16)
        vs_ref[0, h] = vs[rows]


def _proj(x, mod3, weights, b_f, tri, perm):
    B, S, D = x.shape
    full = lambda a: pl.BlockSpec(a.shape, lambda b, s: (0,) * a.ndim)
    head_spec = lambda r: pl.BlockSpec((1, N_HEADS, r, TS), lambda b, s: (b, 0, 0, s))
    flat_spec = pl.BlockSpec((1, WIDTH, TS), lambda b, s: (b, 0, s))
    head_shape = lambda r: jax.ShapeDtypeStruct((B, N_HEADS, r, S), BF16)
    flat_shape = jax.ShapeDtypeStruct((B, WIDTH, S), F32)
    krow_spec = pl.BlockSpec((1, N_HEADS, TS, KPAD), lambda b, s: (b, 0, s, 0))
    krow_shape = jax.ShapeDtypeStruct((B, N_HEADS, S, KPAD), BF16)
    return pl.pallas_call(
        _proj_kernel,
        out_shape=(head_shape(KPAD), krow_shape, head_shape(VROWS), flat_shape,
                   head_shape(KPAD), krow_shape, head_shape(HEAD_DIM), flat_shape),
        grid=(B, S // TS),
        in_specs=[pl.BlockSpec((1, TS, D), lambda b, s: (b, s, 0)),
                  pl.BlockSpec((1, 3, D), lambda b, s: (b, 0, 0))]
                 + [full(w) for w in weights] + [full(b_f), full(tri), full(perm)],
        out_specs=(head_spec(KPAD), krow_spec, head_spec(VROWS), flat_spec,
                   head_spec(KPAD), krow_spec, head_spec(HEAD_DIM), flat_spec),
        scratch_shapes=[pltpu.VMEM((N_HEADS, 128), F32)],
        compiler_params=pltpu.CompilerParams(
            dimension_semantics=("arbitrary", "arbitrary"),
            vmem_limit_bytes=56 * 1024 * 1024),
        name="proj",
    )(x, mod3, *weights, b_f, tri, perm)


def _attention_call(body, name, q, k, v, v_rows, extra_scratch=()):
    B, H, _, S = q.shape
    once = pl.Buffered(1)
    return pl.pallas_call(
        functools.partial(body, seq=S),
        out_shape=jax.ShapeDtypeStruct((B, H * HEAD_DIM, S), F32),
        grid=(B, H // HB),
        in_specs=[pl.BlockSpec((1, HB, KPAD, S), lambda b, h: (b, h, 0, 0), pipeline_mode=once),
                  pl.BlockSpec((1, HB, S, KPAD), lambda b, h: (b, h, 0, 0), pipeline_mode=once),
                  pl.BlockSpec((1, HB, v_rows, S), lambda b, h: (b, h, 0, 0), pipeline_mode=once)],
        out_specs=pl.BlockSpec((1, HB * HEAD_DIM, S), lambda b, h: (b, h, 0)),
        scratch_shapes=[pltpu.VMEM((BK, BQ), F32), pltpu.VMEM((BK, BQ), BF16), *extra_scratch],
        compiler_params=pltpu.CompilerParams(
            dimension_semantics=("arbitrary", "arbitrary"),
            vmem_limit_bytes=56 * 1024 * 1024),
        name=name,
    )(q, k, v)


def _pipeline(scores, weights, values, finish_pending, prefetch, prefetch_step):
    scores(0)
    for step in range(1, HB + 2):
        if step <= HB:
            weights(step - 1)
        if step == 1:
            finish_pending()
        if step >= 2:
            values(step - 2)
        if step < HB:
            scores(step)
        if step == prefetch_step:
            prefetch()


LAST = HB - 1


def _fox_tiles(q_ref, k_ref, v_ref, s_ref, p_ref, qs, ks, ks_next, ks_pend, state, diagonal):
    s, p, out = [None] * HB, [None] * HB, [None] * (2 * HB)
    acc_in = [None] * HB if diagonal else [state[2 * h + 1] for h in range(HB)]

    def finish_pending():
        if not diagonal:
            acc_in[LAST] = acc_in[LAST] + jnp.dot(
                v_ref[0, LAST, :, pl.ds(ks_pend, BK)], p_ref[...], preferred_element_type=F32)

    def prefetch():
        s_ref[...] = jnp.dot(k_ref[0, 0, pl.ds(ks_next, BK), :], q_ref[0, 0, :, pl.ds(qs, BQ)],
                             preferred_element_type=F32)

    def scores(h):
        if h == 0 and not diagonal:
            s[h] = s_ref[...]
            return
        s[h] = jnp.dot(k_ref[0, h, pl.ds(ks, BK), :], q_ref[0, h, :, pl.ds(qs, BQ)],
                       preferred_element_type=F32)
        if diagonal:
            kpos = lax.broadcasted_iota(jnp.int32, (BK, BQ), 0)
            qpos = lax.broadcasted_iota(jnp.int32, (BK, BQ), 1)
            s[h] = jnp.where(kpos <= qpos, s[h], -jnp.inf)

    def softmax(h):
        mb = jnp.max(s[h], axis=0, keepdims=True)
        out[2 * h] = mb if diagonal else jnp.maximum(state[2 * h], mb)
        p[h] = jnp.exp2(s[h] - out[2 * h]).astype(BF16)

    def values(h):
        if diagonal:
            scaled = None
        else:
            scaled = jnp.exp2(state[2 * h] - out[2 * h]) * acc_in[h]
        if h == LAST:
            p_ref[...] = p[h]
            out[2 * h + 1] = jnp.zeros((VROWS, BQ), F32) if diagonal else scaled
        else:
            pv = jnp.dot(v_ref[0, h, :, pl.ds(ks, BK)], p[h], preferred_element_type=F32)
            out[2 * h + 1] = pv if diagonal else scaled + pv

    _pipeline(scores, softmax, values, finish_pending, prefetch, HB - 1)
    return tuple(out)


def _fox_kernel(q_ref, k_ref, v_ref, o_ref, s_ref, p_ref, *, seq):
    def qbody(qi, carry):
        qs = pl.multiple_of(qi * BQ, BQ)
        state = _fox_tiles(q_ref, k_ref, v_ref, s_ref, p_ref, qs, qs, 0, None, None, True)

        def kbody(kj, carry):
            st, ks_pend = carry
            ks_pend = pl.multiple_of(ks_pend, BK)
            ks = pl.multiple_of(kj * BK, BK)
            ks_next = pl.multiple_of(jnp.minimum(kj + 1, qi) * BK, BK)
            return _fox_tiles(q_ref, k_ref, v_ref, s_ref, p_ref, qs, ks, ks_next, ks_pend,
                              st, False), ks

        st, ks_pend = lax.fori_loop(0, qi, kbody, (state, qs))
        for h in range(HB):
            acc = st[2 * h + 1]
            if h == LAST:
                acc = acc + jnp.dot(v_ref[0, LAST, :, pl.ds(pl.multiple_of(ks_pend, BK), BK)],
                                    p_ref[...], preferred_element_type=F32)
            o_ref[0, h * HEAD_DIM:(h + 1) * HEAD_DIM, pl.ds(qs, BQ)] = (
                acc[0:HEAD_DIM] / acc[HEAD_DIM:HEAD_DIM + 1])
        return carry

    lax.fori_loop(0, seq // BQ, qbody, 0)


def _suffix_prod_sublanes(t):
    r = lax.broadcasted_iota(jnp.int32, t.shape, 0)
    incl = t
    for d in (1, 2, 4):
        incl = incl * jnp.where(r + d < STRANDS, pltpu.roll(incl, STRANDS - d, axis=0), 1.0)
    excl = jnp.where(r + 1 < STRANDS, pltpu.roll(incl, STRANDS - 1, axis=0), 1.0)
    return excl, incl


def _sb_weights(z, keep_ref, carry, diagonal):
    keep = 1.0 / (1.0 + jnp.exp2(z))
    if diagonal:
        row = lax.broadcasted_iota(jnp.int32, (BK, BQ), 0)
        sub_row = row % SUB
        kpos = (row // SUB) * SUB + (sub_row % STRANDS) * STRAND_LEN + sub_row // STRANDS
        strict = kpos < lax.broadcasted_iota(jnp.int32, (BK, BQ), 1)
        keep = jnp.where(strict, keep, 1.0)
    keep_ref[...] = keep
    n_sub = BK // SUB
    rows = lambda sub, i: keep_ref[pl.ds(sub * SUB + i * STRANDS, STRANDS), :]
    ws = [[None] * STRAND_LEN for _ in range(n_sub)]
    for sub in reversed(range(n_sub)):
        total = rows(sub, 0)
        for i in range(1, STRAND_LEN):
            total = total * rows(sub, i)
        excl, incl = _suffix_prod_sublanes(total)
        run = carry * excl
        carry = carry * jnp.broadcast_to(incl[0:1], carry.shape)
        for i in reversed(range(STRAND_LEN)):
            nxt = run * rows(sub, i)
            ws[sub][i] = run - nxt
            run = nxt
    w = jnp.concatenate([x for sub_ws in ws for x in sub_ws], axis=0)
    return w.astype(BF16), carry


def _sb_tiles(q_ref, k_ref, v_ref, s_ref, p_ref, keep_ref, qs, ks, ks_next, ks_pend, state,
              diagonal):
    z, w, out = [None] * HB, [None] * HB, [None] * (2 * HB)
    acc_in = [None] * HB if diagonal else [state[2 * h + 1] for h in range(HB)]

    def finish_pending():
        if not diagonal:
            acc_in[LAST] = acc_in[LAST] + jnp.dot(
                v_ref[0, LAST, :, pl.ds(ks_pend, BK)], p_ref[...], preferred_element_type=F32)

    def prefetch():
        s_ref[...] = jnp.dot(k_ref[0, 0, pl.ds(ks_next, BK), :], q_ref[0, 0, :, pl.ds(qs, BQ)],
                             preferred_element_type=F32)

    def scores(h):
        if h == 0 and not diagonal:
            z[h] = s_ref[...]
        else:
            z[h] = jnp.dot(k_ref[0, h, pl.ds(ks, BK), :], q_ref[0, h, :, pl.ds(qs, BQ)],
                           preferred_element_type=F32)

    def weights(h):
        carry = jnp.ones((STRANDS, BQ), F32) if diagonal else state[2 * h]
        w[h], out[2 * h] = _sb_weights(z[h], keep_ref.at[h % 2], carry, diagonal)

    def values(h):
        if h == LAST:
            p_ref[...] = w[h]
            out[2 * h + 1] = jnp.zeros((HEAD_DIM, BQ), F32) if diagonal else acc_in[h]
        else:
            pv = jnp.dot(v_ref[0, h, :, pl.ds(ks, BK)], w[h], preferred_element_type=F32)
            out[2 * h + 1] = pv if diagonal else acc_in[h] + pv

    _pipeline(scores, weights, values, finish_pending, prefetch, HB)
    return tuple(out)


def _sb_kernel(q_ref, k_ref, v_ref, o_ref, s_ref, p_ref, keep_ref, *, seq):
    def qbody(qi, carry):
        qs = pl.multiple_of(qi * BQ, BQ)
        first = pl.multiple_of(jnp.maximum(qi - 1, 0) * BK, BK)
        state = _sb_tiles(q_ref, k_ref, v_ref, s_ref, p_ref, keep_ref, qs, qs, first, None, None,
                          True)

        def kbody(t, carry):
            st, ks_pend = carry
            ks_pend = pl.multiple_of(ks_pend, BK)
            j = qi - 1 - t
            ks = pl.multiple_of(j * BK, BK)
            ks_next = pl.multiple_of(jnp.maximum(j - 1, 0) * BK, BK)
            return _sb_tiles(q_ref, k_ref, v_ref, s_ref, p_ref, keep_ref, qs, ks, ks_next,
                             ks_pend, st, False), ks

        st, ks_pend = lax.fori_loop(0, qi, kbody, (state, qs))
        for h in range(HB):
            acc = st[2 * h + 1]
            if h == LAST:
                acc = acc + jnp.dot(v_ref[0, LAST, :, pl.ds(pl.multiple_of(ks_pend, BK), BK)],
                                    p_ref[...], preferred_element_type=F32)
            o_ref[0, h * HEAD_DIM:(h + 1) * HEAD_DIM, pl.ds(qs, BQ)] = acc
        return carry

    lax.fori_loop(0, seq // BQ, qbody, 0)


def _out_kernel(yf_ref, ys_ref, gf_ref, gs_ref, x_ref, mod_ref, w_ref, lng_ref, lnb_ref, o_ref):
    yg = jnp.concatenate([yf_ref[0] * gf_ref[0], ys_ref[0] * gs_ref[0]], axis=0)
    y = jnp.dot(yg.T.astype(BF16), w_ref[...], preferred_element_type=F32)
    gate = mod_ref[0, 2:3, :]
    resid = DEEPNORM_ALPHA * x_ref[0] + gate * y
    mu = jnp.mean(resid, axis=-1, keepdims=True)
    rc = resid - mu
    var = jnp.mean(rc * rc, axis=-1, keepdims=True)
    o_ref[0] = rc * lax.rsqrt(var + LN_EPS) * lng_ref[...] + lnb_ref[...]


def _out(yf, ys, gf, gs, x, mod3, w_out, ln_g, ln_b):
    B, S, D = x.shape
    flat_spec = pl.BlockSpec((1, WIDTH, TS), lambda b, s: (b, 0, s))
    full = lambda a: pl.BlockSpec(a.shape, lambda b, s: (0,) * a.ndim)
    return pl.pallas_call(
        _out_kernel,
        out_shape=jax.ShapeDtypeStruct((B, S, D), F32),
        grid=(B, S // TS),
        in_specs=[flat_spec, flat_spec, flat_spec, flat_spec,
                  pl.BlockSpec((1, TS, D), lambda b, s: (b, s, 0)),
                  pl.BlockSpec((1, 3, D), lambda b, s: (b, 0, 0)),
                  full(w_out), full(ln_g), full(ln_b)],
        out_specs=pl.BlockSpec((1, TS, D), lambda b, s: (b, s, 0)),
        compiler_params=pltpu.CompilerParams(
            dimension_semantics=("arbitrary", "arbitrary"),
            vmem_limit_bytes=48 * 1024 * 1024),
        name="out",
    )(yf, ys, gf, gs, x, mod3, w_out, ln_g, ln_b)


def kernel(x, c, w_ada, b_ada, w_in, b_f, w_out, ln_g, ln_b):
    B, S, D = x.shape
    assert D == D_MODEL and S % TS == 0 and w_ada.shape[0] == DEPTH
    tri = (jnp.arange(TS)[:, None] <= jnp.arange(TS)[None, :]).astype(BF16)
    p = jnp.arange(TS)
    src = (p // SUB) * SUB + (p % STRANDS) * STRAND_LEN + (p % SUB) // STRANDS
    perm = (src[:, None] == jnp.arange(TS)[None, :]).astype(BF16)
    for layer in range(DEPTH):
        c8 = jnp.pad(c, ((0, 8 - B), (0, 0)))
        mod = _ada(c8, w_ada[layer], b_ada[layer][None, :])[:B]
        mod3 = mod.reshape(B, 3, D)
        wt = w_in[layer].T.astype(BF16)
        o = 0
        parts = []
        for wdt in (WIDTH, WIDTH, WIDTH, WIDTH, N_HEADS, WIDTH, WIDTH, WIDTH, WIDTH):
            parts.append(wt[o:o + wdt])
            o += wdt
        wfq, wfk, wfv, wfg, wff, wsq, wsk, wsv, wsg = parts
        wff = jnp.pad(wff, ((0, 16 - N_HEADS), (0, 0)))
        weights = (wfq, wfk, wfv, wfg, wff, wsq, wsk, wsv, wsg)
        qf, kf, vf, gf, qs, ks, vs, gs = _proj(x, mod3, weights, b_f[layer][:, None], tri, perm)
        yf = _attention_call(_fox_kernel, "fox", qf, kf, vf, VROWS)
        ys = _attention_call(_sb_kernel, "sb", qs, ks, vs, HEAD_DIM,
                             extra_scratch=[pltpu.VMEM((2, BK, BQ), F32)])
        x = _out(yf, ys, gf, gs, x, mod3, w_out[layer].astype(BF16),
                 ln_g[layer][None, :], ln_b[layer][None, :])
    return x
```

```python
import functools

import jax
import jax.numpy as jnp
from jax import lax
from jax.experimental import pallas as pl
from jax.experimental.pallas import tpu as pltpu

D_MODEL = 1024
HEAD_DIM = 64
N_HEADS = 8
WIDTH = N_HEADS * HEAD_DIM
LN_EPS = 1e-5
DEPTH = 1
DEEPNORM_ALPHA = (2.0 * DEPTH) ** 0.25

TS = 512
BQ = 512
BK = 512
HB = 4
SUB = 256
STRANDS = 8
STRAND_LEN = SUB // STRANDS
KPAD = 128
VROWS = 80
LOG2E = 1.4426950408889634

F32 = jnp.float32
BF16 = jnp.bfloat16
NT_DIMS = (((1,), (1,)), ((), ()))


def _split3(f):
    hi = f.astype(BF16)
    r1 = f - hi.astype(F32)
    mid = r1.astype(BF16)
    lo = (r1 - mid.astype(F32)).astype(BF16)
    return hi, mid, lo


def _ada_kernel(c_ref, w_ref, b_ref, o_ref):
    c = c_ref[...]
    a = c * jax.nn.sigmoid(c)
    o_ref[...] = jnp.dot(a, w_ref[...], preferred_element_type=F32,
                         precision=lax.Precision.HIGHEST) + b_ref[...]


def _ada(c8, w_ada, b_ada):
    n = w_ada.shape[1]
    return pl.pallas_call(
        _ada_kernel,
        out_shape=jax.ShapeDtypeStruct((8, n), F32),
        grid=(n // D_MODEL,),
        in_specs=[pl.BlockSpec((8, D_MODEL), lambda j: (0, 0)),
                  pl.BlockSpec((D_MODEL, D_MODEL), lambda j: (0, j)),
                  pl.BlockSpec((1, D_MODEL), lambda j: (0, j))],
        out_specs=pl.BlockSpec((8, D_MODEL), lambda j: (0, j)),
        name="ada",
    )(c8, w_ada, b_ada)


def _proj_kernel(x_ref, mod_ref, wfq_ref, wfk_ref, wfv_ref, wfg_ref, wff_ref,
                 wsq_ref, wsk_ref, wsv_ref, wsg_ref, bf_ref, tri_ref, perm_ref,
                 qf_ref, kf_ref, vf_ref, gf_ref, qs_ref, ks_ref, vs_ref, gs_ref,
                 carry_ref):
    @pl.when(pl.program_id(1) == 0)
    def _():
        carry_ref[...] = jnp.zeros_like(carry_ref)

    x = x_ref[0]
    mu = jnp.mean(x, axis=-1, keepdims=True)
    xc = x - mu
    var = jnp.mean(xc * xc, axis=-1, keepdims=True)
    shift = mod_ref[0, 0:1, :]
    scale = mod_ref[0, 1:2, :]
    u = xc * lax.rsqrt(var + LN_EPS) * (1.0 + scale) + shift
    ub = u.astype(BF16)
    up = jnp.dot(perm_ref[...], ub, preferred_element_type=F32).astype(BF16)

    def proj(w_ref, lhs):
        return lax.dot_general(w_ref[...], lhs, NT_DIMS, preferred_element_type=F32)

    ff = proj(wff_ref, ub)[0:N_HEADS]
    logf = jax.nn.log_sigmoid(ff + bf_ref[...])
    local = jnp.zeros((N_HEADS, TS), F32)
    for part in _split3(logf):
        local = local + jnp.dot(part, tri_ref[...], preferred_element_type=F32)
    fcum = local + carry_ref[:, 0:1]
    carry_ref[...] = jnp.broadcast_to(fcum[:, TS - 1:TS], carry_ref.shape)
    nsplit = [t.astype(F32) for t in _split3(-LOG2E * fcum)]

    zeros_pad = jnp.zeros((KPAD - HEAD_DIM - 16, TS), BF16)
    zeros_half = jnp.zeros((KPAD - HEAD_DIM, TS), BF16)
    ones3 = (lax.broadcasted_iota(jnp.int32, (16, TS), 0) < 3).astype(BF16)
    ones1 = (lax.broadcasted_iota(jnp.int32, (16, TS), 0) < 1).astype(BF16)

    qf = (proj(wfq_ref, ub) * (LOG2E * HEAD_DIM ** -0.5)).astype(BF16)
    kf = proj(wfk_ref, ub)
    vf = proj(wfv_ref, ub).astype(BF16)
    g = proj(wfg_ref, ub)
    gf_ref[0] = g * jax.nn.sigmoid(g)
    for h in range(N_HEADS):
        rows = slice(h * HEAD_DIM, (h + 1) * HEAD_DIM)
        qf_ref[0, h, 0:HEAD_DIM, :] = qf[rows]
        qf_ref[0, h, HEAD_DIM:HEAD_DIM + 16, :] = ones3
        qf_ref[0, h, HEAD_DIM + 16:KPAD, :] = zeros_pad
        kaug = jnp.concatenate(
            [kf[rows]] + [t[h:h + 1] for t in nsplit]
            + [jnp.zeros((KPAD - HEAD_DIM - 3, TS), F32)], axis=0)
        kf_ref[0, h] = kaug.T.astype(BF16)
        vf_ref[0, h, 0:HEAD_DIM, :] = vf[rows]
        vf_ref[0, h, HEAD_DIM:VROWS, :] = ones1

    qs = (proj(wsq_ref, ub) * (0.5 * HEAD_DIM ** -0.5)).astype(BF16)
    ks = proj(wsk_ref, up)
    vs = proj(wsv_ref, up).astype(BF16)
    g = proj(wsg_ref, ub)
    gs_ref[0] = g * jax.nn.sigmoid(g)
    for h in range(N_HEADS):
        rows = slice(h * HEAD_DIM, (h + 1) * HEAD_DIM)
        qs_ref[0, h, 0:HEAD_DIM, :] = qs[rows]
        qs_ref[0, h, HEAD_DIM:KPAD, :] = zeros_half
        kpad = jnp.concatenate([ks[rows], jnp.zeros((KPAD - HEAD_DIM, TS), F32)], axis=0)
        ks_ref[0, h] = kpad.T.astype(BF16)
        vs_ref[0, h] = vs[rows]


def _proj(x, mod3, weights, b_f, tri, perm):
    B, S, D = x.shape
    full = lambda a: pl.BlockSpec(a.shape, lambda b, s: (0,) * a.ndim)
    head_spec = lambda r: pl.BlockSpec((1, N_HEADS, r, TS), lambda b, s: (b, 0, 0, s))
    flat_spec = pl.BlockSpec((1, WIDTH, TS), lambda b, s: (b, 0, s))
    head_shape = lambda r: jax.ShapeDtypeStruct((B, N_HEADS, r, S), BF16)
    flat_shape = jax.ShapeDtypeStruct((B, WIDTH, S), F32)
    krow_spec = pl.BlockSpec((1, N_HEADS, TS, KPAD), lambda b, s: (b, 0, s, 0))
    krow_shape = jax.ShapeDtypeStruct((B, N_HEADS, S, KPAD), BF16)
    return pl.pallas_call(
        _proj_kernel,
        out_shape=(head_shape(KPAD), krow_shape, head_shape(VROWS), flat_shape,
                   head_shape(KPAD), krow_shape, head_shape(HEAD_DIM), flat_shape),
        grid=(B, S // TS),
        in_specs=[pl.BlockSpec((1, TS, D), lambda b, s: (b, s, 0)),
                  pl.BlockSpec((1, 3, D), lambda b, s: (b, 0, 0))]
                 + [full(w) for w in weights] + [full(b_f), full(tri), full(perm)],
        out_specs=(head_spec(KPAD), krow_spec, head_spec(VROWS), flat_spec,
                   head_spec(KPAD), krow_spec, head_spec(HEAD_DIM), flat_spec),
        scratch_shapes=[pltpu.VMEM((N_HEADS, 128), F32)],
        compiler_params=pltpu.CompilerParams(
            dimension_semantics=("arbitrary", "arbitrary"),
            vmem_limit_bytes=56 * 1024 * 1024),
        name="proj",
    )(x, mod3, *weights, b_f, tri, perm)


def _attention_call(body, name, q, k, v, v_rows, extra_scratch=()):
    B, H, _, S = q.shape
    once = pl.Buffered(1)
    return pl.pallas_call(
        functools.partial(body, seq=S),
        out_shape=jax.ShapeDtypeStruct((B, H * HEAD_DIM, S), F32),
        grid=(B, H // HB),
        in_specs=[pl.BlockSpec((1, HB, KPAD, S), lambda b, h: (b, h, 0, 0), pipeline_mode=once),
                  pl.BlockSpec((1, HB, S, KPAD), lambda b, h: (b, h, 0, 0), pipeline_mode=once),
                  pl.BlockSpec((1, HB, v_rows, S), lambda b, h: (b, h, 0, 0), pipeline_mode=once)],
        out_specs=pl.BlockSpec((1, HB * HEAD_DIM, S), lambda b, h: (b, h, 0)),
        scratch_shapes=[pltpu.VMEM((BK, BQ), F32), pltpu.VMEM((BK, BQ), BF16), *extra_scratch],
        compiler_params=pltpu.CompilerParams(
            dimension_semantics=("arbitrary", "arbitrary"),
            vmem_limit_bytes=56 * 1024 * 1024),
        name=name,
    )(q, k, v)


def _pipeline(scores, weights, values, finish_pending, prefetch, prefetch_step):
    scores(0)
    for step in range(1, HB + 2):
        if step <= HB:
            weights(step - 1)
        if step == 1:
            finish_pending()
        if step >= 2:
            values(step - 2)
        if step < HB:
            scores(step)
        if step == prefetch_step:
            prefetch()


LAST = HB - 1


def _fox_tiles(q_ref, k_ref, v_ref, s_ref, p_ref, qs, ks, ks_next, ks_pend, state, mode):
    diagonal = mode == "diagonal"
    s, p, out = [None] * HB, [None] * HB, [None] * (2 * HB)
    acc_in = [None] * HB if diagonal else [state[2 * h + 1] for h in range(HB)]

    def finish_pending():
        if not diagonal:
            acc_in[LAST] = acc_in[LAST] + jnp.dot(
                v_ref[0, LAST, :, pl.ds(ks_pend, BK)], p_ref[...], preferred_element_type=F32)

    def prefetch():
        s_ref[...] = jnp.dot(k_ref[0, 0, pl.ds(ks_next, BK), :], q_ref[0, 0, :, pl.ds(qs, BQ)],
                             preferred_element_type=F32)

    def scores(h):
        if h == 0 and not diagonal:
            s[h] = s_ref[...]
            return
        s[h] = jnp.dot(k_ref[0, h, pl.ds(ks, BK), :], q_ref[0, h, :, pl.ds(qs, BQ)],
                       preferred_element_type=F32)
        if diagonal:
            kpos = lax.broadcasted_iota(jnp.int32, (BK, BQ), 0)
            qpos = lax.broadcasted_iota(jnp.int32, (BK, BQ), 1)
            s[h] = jnp.where(kpos <= qpos, s[h], -jnp.inf)

    def softmax(h):
        if mode == "frozen":
            out[2 * h] = state[2 * h]
        else:
            mb = jnp.max(s[h], axis=0, keepdims=True)
            out[2 * h] = mb if diagonal else jnp.maximum(state[2 * h], mb)
        p[h] = jnp.exp2(s[h] - out[2 * h]).astype(BF16)

    def values(h):
        if mode == "online":
            scaled = jnp.exp2(state[2 * h] - out[2 * h]) * acc_in[h]
        else:
            scaled = acc_in[h]
        if h == LAST:
            p_ref[...] = p[h]
            out[2 * h + 1] = jnp.zeros((VROWS, BQ), F32) if diagonal else scaled
        else:
            pv = jnp.dot(v_ref[0, h, :, pl.ds(ks, BK)], p[h], preferred_element_type=F32)
            out[2 * h + 1] = pv if diagonal else scaled + pv

    _pipeline(scores, softmax, values, finish_pending, prefetch, HB - 1)
    return tuple(out)


def _fox_kernel(q_ref, k_ref, v_ref, o_ref, s_ref, p_ref, *, seq):
    def qbody(qi, carry):
        qs = pl.multiple_of(qi * BQ, BQ)

        def run(mode):
            state = _fox_tiles(q_ref, k_ref, v_ref, s_ref, p_ref, qs, qs, 0, None, None,
                               "diagonal")

            def kbody(kj, carry):
                st, ks_pend = carry
                ks_pend = pl.multiple_of(ks_pend, BK)
                ks = pl.multiple_of(kj * BK, BK)
                ks_next = pl.multiple_of(jnp.minimum(kj + 1, qi) * BK, BK)
                return _fox_tiles(q_ref, k_ref, v_ref, s_ref, p_ref, qs, ks, ks_next, ks_pend,
                                  st, mode), ks

            st, ks_pend = lax.fori_loop(0, qi, kbody, (state, qs))
            accs = [st[2 * h + 1] for h in range(HB)]
            accs[LAST] = accs[LAST] + jnp.dot(
                v_ref[0, LAST, :, pl.ds(pl.multiple_of(ks_pend, BK), BK)], p_ref[...],
                preferred_element_type=F32)
            return accs

        def store(accs):
            for h in range(HB):
                o_ref[0, h * HEAD_DIM:(h + 1) * HEAD_DIM, pl.ds(qs, BQ)] = (
                    accs[h][0:HEAD_DIM] / accs[h][HEAD_DIM:HEAD_DIM + 1])

        accs = run("frozen")
        overflow = jnp.zeros((1, BQ), F32)
        for acc in accs:
            bad = jnp.where(jnp.abs(acc) < jnp.inf, 0.0, 1.0)
            overflow = jnp.maximum(overflow, jnp.max(bad, axis=0, keepdims=True))
        store(accs)

        @pl.when(jnp.max(overflow) > 0.0)
        def _():
            store(run("online"))

        return carry

    lax.fori_loop(0, seq // BQ, qbody, 0)


def _suffix_prod_sublanes(t):
    r = lax.broadcasted_iota(jnp.int32, t.shape, 0)
    incl = t
    for d in (1, 2, 4):
        incl = incl * jnp.where(r + d < STRANDS, pltpu.roll(incl, STRANDS - d, axis=0), 1.0)
    excl = jnp.where(r + 1 < STRANDS, pltpu.roll(incl, STRANDS - 1, axis=0), 1.0)
    return excl, incl


def _sb_weights(z, keep_ref, carry, diagonal):
    keep = 0.5 - 0.5 * jnp.tanh(z)
    if diagonal:
        row = lax.broadcasted_iota(jnp.int32, (BK, BQ), 0)
        sub_row = row % SUB
        kpos = (row // SUB) * SUB + (sub_row % STRANDS) * STRAND_LEN + sub_row // STRANDS
        strict = kpos < lax.broadcasted_iota(jnp.int32, (BK, BQ), 1)
        keep = jnp.where(strict, keep, 1.0)
    keep_ref[...] = keep
    n_sub = BK // SUB
    rows = lambda sub, i: keep_ref[pl.ds(sub * SUB + i * STRANDS, STRANDS), :]
    ws = [[None] * STRAND_LEN for _ in range(n_sub)]
    for sub in reversed(range(n_sub)):
        total = rows(sub, 0)
        for i in range(1, STRAND_LEN):
            total = total * rows(sub, i)
        excl, incl = _suffix_prod_sublanes(total)
        run = carry * excl
        carry = carry * jnp.broadcast_to(incl[0:1], carry.shape)
        for i in reversed(range(STRAND_LEN)):
            nxt = run * rows(sub, i)
            ws[sub][i] = run - nxt
            run = nxt
    w = jnp.concatenate([x for sub_ws in ws for x in sub_ws], axis=0)
    return w.astype(BF16), carry


def _sb_tiles(q_ref, k_ref, v_ref, s_ref, p_ref, keep_ref, qs, ks, ks_next, ks_pend, state,
              diagonal):
    z, w, out = [None] * HB, [None] * HB, [None] * (2 * HB)
    acc_in = [None] * HB if diagonal else [state[2 * h + 1] for h in range(HB)]

    def finish_pending():
        if not diagonal:
            acc_in[LAST] = acc_in[LAST] + jnp.dot(
                v_ref[0, LAST, :, pl.ds(ks_pend, BK)], p_ref[...], preferred_element_type=F32)

    def prefetch():
        s_ref[...] = jnp.dot(k_ref[0, 0, pl.ds(ks_next, BK), :], q_ref[0, 0, :, pl.ds(qs, BQ)],
                             preferred_element_type=F32)

    def scores(h):
        if h == 0 and not diagonal:
            z[h] = s_ref[...]
        else:
            z[h] = jnp.dot(k_ref[0, h, pl.ds(ks, BK), :], q_ref[0, h, :, pl.ds(qs, BQ)],
                           preferred_element_type=F32)

    def weights(h):
        carry = jnp.ones((STRANDS, BQ), F32) if diagonal else state[2 * h]
        w[h], out[2 * h] = _sb_weights(z[h], keep_ref.at[h % 2], carry, diagonal)

    def values(h):
        if h == LAST:
            p_ref[...] = w[h]
            out[2 * h + 1] = jnp.zeros((HEAD_DIM, BQ), F32) if diagonal else acc_in[h]
        else:
            pv = jnp.dot(v_ref[0, h, :, pl.ds(ks, BK)], w[h], preferred_element_type=F32)
            out[2 * h + 1] = pv if diagonal else acc_in[h] + pv

    _pipeline(scores, weights, values, finish_pending, prefetch, HB)
    return tuple(out)


def _sb_kernel(q_ref, k_ref, v_ref, o_ref, s_ref, p_ref, keep_ref, *, seq):
    def qbody(qi, carry):
        qs = pl.multiple_of(qi * BQ, BQ)
        first = pl.multiple_of(jnp.maximum(qi - 1, 0) * BK, BK)
        state = _sb_tiles(q_ref, k_ref, v_ref, s_ref, p_ref, keep_ref, qs, qs, first, None, None,
                          True)

        def kbody(t, carry):
            st, ks_pend = carry
            ks_pend = pl.multiple_of(ks_pend, BK)
            j = qi - 1 - t
            ks = pl.multiple_of(j * BK, BK)
            ks_next = pl.multiple_of(jnp.maximum(j - 1, 0) * BK, BK)
            return _sb_tiles(q_ref, k_ref, v_ref, s_ref, p_ref, keep_ref, qs, ks, ks_next,
                             ks_pend, st, False), ks

        st, ks_pend = lax.fori_loop(0, qi, kbody, (state, qs))
        for h in range(HB):
            acc = st[2 * h + 1]
            if h == LAST:
                acc = acc + jnp.dot(v_ref[0, LAST, :, pl.ds(pl.multiple_of(ks_pend, BK), BK)],
                                    p_ref[...], preferred_element_type=F32)
            o_ref[0, h * HEAD_DIM:(h + 1) * HEAD_DIM, pl.ds(qs, BQ)] = acc
        return carry

    lax.fori_loop(0, seq // BQ, qbody, 0)


def _out_kernel(yf_ref, ys_ref, gf_ref, gs_ref, x_ref, mod_ref, w_ref, lng_ref, lnb_ref, o_ref):
    yg = jnp.concatenate([yf_ref[0] * gf_ref[0], ys_ref[0] * gs_ref[0]], axis=0)
    y = jnp.dot(yg.T.astype(BF16), w_ref[...], preferred_element_type=F32)
    gate = mod_ref[0, 2:3, :]
    resid = DEEPNORM_ALPHA * x_ref[0] + gate * y
    mu = jnp.mean(resid, axis=-1, keepdims=True)
    rc = resid - mu
    var = jnp.mean(rc * rc, axis=-1, keepdims=True)
    o_ref[0] = rc * lax.rsqrt(var + LN_EPS) * lng_ref[...] + lnb_ref[...]


def _out(yf, ys, gf, gs, x, mod3, w_out, ln_g, ln_b):
    B, S, D = x.shape
    flat_spec = pl.BlockSpec((1, WIDTH, TS), lambda b, s: (b, 0, s))
    full = lambda a: pl.BlockSpec(a.shape, lambda b, s: (0,) * a.ndim)
    return pl.pallas_call(
        _out_kernel,
        out_shape=jax.ShapeDtypeStruct((B, S, D), F32),
        grid=(B, S // TS),
        in_specs=[flat_spec, flat_spec, flat_spec, flat_spec,
                  pl.BlockSpec((1, TS, D), lambda b, s: (b, s, 0)),
                  pl.BlockSpec((1, 3, D), lambda b, s: (b, 0, 0)),
                  full(w_out), full(ln_g), full(ln_b)],
        out_specs=pl.BlockSpec((1, TS, D), lambda b, s: (b, s, 0)),
        compiler_params=pltpu.CompilerParams(
            dimension_semantics=("arbitrary", "arbitrary"),
            vmem_limit_bytes=48 * 1024 * 1024),
        name="out",
    )(yf, ys, gf, gs, x, mod3, w_out, ln_g, ln_b)


def kernel(x, c, w_ada, b_ada, w_in, b_f, w_out, ln_g, ln_b):
    B, S, D = x.shape
    assert D == D_MODEL and S % TS == 0 and w_ada.shape[0] == DEPTH
    tri = (jnp.arange(TS)[:, None] <= jnp.arange(TS)[None, :]).astype(BF16)
    p = jnp.arange(TS)
    src = (p // SUB) * SUB + (p % STRANDS) * STRAND_LEN + (p % SUB) // STRANDS
    perm = (src[:, None] == jnp.arange(TS)[None, :]).astype(BF16)
    for layer in range(DEPTH):
        c8 = jnp.pad(c, ((0, 8 - B), (0, 0)))
        mod = _ada(c8, w_ada[layer], b_ada[layer][None, :])[:B]
        mod3 = mod.reshape(B, 3, D)
        wt = w_in[layer].T.astype(BF16)
        o = 0
        parts = []
        for wdt in (WIDTH, WIDTH, WIDTH, WIDTH, N_HEADS, WIDTH, WIDTH, WIDTH, WIDTH):
            parts.append(wt[o:o + wdt])
            o += wdt
        wfq, wfk, wfv, wfg, wff, wsq, wsk, wsv, wsg = parts
        wff = jnp.pad(wff, ((0, 16 - N_HEADS), (0, 0)))
        weights = (wfq, wfk, wfv, wfg, wff, wsq, wsk, wsv, wsg)
        qf, kf, vf, gf, qs, ks, vs, gs = _proj(x, mod3, weights, b_f[layer][:, None], tri, perm)
        yf = _attention_call(_fox_kernel, "fox", qf, kf, vf, VROWS)
        ys = _attention_call(_sb_kernel, "sb", qs, ks, vs, HEAD_DIM,
                             extra_scratch=[pltpu.VMEM((2, BK, BQ), F32)])
        x = _out(yf, ys, gf, gs, x, mod3, w_out[layer].astype(BF16),
                 ln_g[layer][None, :], ln_b[layer][None, :])
    return x
```

```python
import functools

import jax
import jax.numpy as jnp
from jax import lax
from jax.experimental import pallas as pl
from jax.experimental.pallas import tpu as pltpu

D_MODEL = 1024
HEAD_DIM = 64
N_HEADS = 8
WIDTH = N_HEADS * HEAD_DIM
LN_EPS = 1e-5
DEPTH = 1
DEEPNORM_ALPHA = (2.0 * DEPTH) ** 0.25

TS = 512
BQ = 512
BK = 512
HP = 2
PREFETCH_STEP = 2
SUB = 256
STRANDS = 8
STRAND_LEN = SUB // STRANDS
KPAD = 128
VROWS = 80
LOG2E = 1.4426950408889634

F32 = jnp.float32
BF16 = jnp.bfloat16
NT_DIMS = (((1,), (1,)), ((), ()))


def _split3(f):
    hi = f.astype(BF16)
    r1 = f - hi.astype(F32)
    mid = r1.astype(BF16)
    lo = (r1 - mid.astype(F32)).astype(BF16)
    return hi, mid, lo


def _ada_kernel(c_ref, w_ref, b_ref, o_ref):
    c = c_ref[...]
    a = c * jax.nn.sigmoid(c)
    o_ref[...] = jnp.dot(a, w_ref[...], preferred_element_type=F32,
                         precision=lax.Precision.HIGHEST) + b_ref[...]


def _ada(c8, w_ada, b_ada):
    n = w_ada.shape[1]
    return pl.pallas_call(
        _ada_kernel,
        out_shape=jax.ShapeDtypeStruct((8, n), F32),
        grid=(n // D_MODEL,),
        in_specs=[pl.BlockSpec((8, D_MODEL), lambda j: (0, 0)),
                  pl.BlockSpec((D_MODEL, D_MODEL), lambda j: (0, j)),
                  pl.BlockSpec((1, D_MODEL), lambda j: (0, j))],
        out_specs=pl.BlockSpec((8, D_MODEL), lambda j: (0, j)),
        name="ada",
    )(c8, w_ada, b_ada)


def _proj_kernel(x_ref, mod_ref, wfq_ref, wfk_ref, wfv_ref, wfg_ref, wff_ref,
                 wsq_ref, wsk_ref, wsv_ref, wsg_ref, bf_ref, tri_ref, perm_ref,
                 qf_ref, kf_ref, vf_ref, gf_ref, qs_ref, ks_ref, vs_ref, gs_ref,
                 carry_ref):
    @pl.when(pl.program_id(1) == 0)
    def _():
        carry_ref[...] = jnp.zeros_like(carry_ref)

    x = x_ref[0]
    mu = jnp.mean(x, axis=-1, keepdims=True)
    xc = x - mu
    var = jnp.mean(xc * xc, axis=-1, keepdims=True)
    shift = mod_ref[0, 0:1, :]
    scale = mod_ref[0, 1:2, :]
    u = xc * lax.rsqrt(var + LN_EPS) * (1.0 + scale) + shift
    ub = u.astype(BF16)
    up = jnp.dot(perm_ref[...], ub, preferred_element_type=F32).astype(BF16)

    def proj(w_ref, lhs):
        return lax.dot_general(w_ref[...], lhs, NT_DIMS, preferred_element_type=F32)

    ff = proj(wff_ref, ub)[0:N_HEADS]
    logf = jax.nn.log_sigmoid(ff + bf_ref[...])
    local = jnp.zeros((N_HEADS, TS), F32)
    for part in _split3(logf):
        local = local + jnp.dot(part, tri_ref[...], preferred_element_type=F32)
    fcum = local + carry_ref[:, 0:1]
    carry_ref[...] = jnp.broadcast_to(fcum[:, TS - 1:TS], carry_ref.shape)
    nsplit = [t.astype(F32) for t in _split3(-LOG2E * fcum)]

    zeros_pad = jnp.zeros((KPAD - HEAD_DIM - 16, TS), BF16)
    zeros_half = jnp.zeros((KPAD - HEAD_DIM, TS), BF16)
    ones3 = (lax.broadcasted_iota(jnp.int32, (16, TS), 0) < 3).astype(BF16)
    ones1 = (lax.broadcasted_iota(jnp.int32, (16, TS), 0) < 1).astype(BF16)

    qf = (proj(wfq_ref, ub) * (LOG2E * HEAD_DIM ** -0.5)).astype(BF16)
    kf = proj(wfk_ref, ub)
    vf = proj(wfv_ref, ub).astype(BF16)
    g = proj(wfg_ref, ub)
    gf_ref[0] = g * jax.nn.sigmoid(g)
    for h in range(N_HEADS):
        rows = slice(h * HEAD_DIM, (h + 1) * HEAD_DIM)
        qf_ref[0, h, 0:HEAD_DIM, :] = qf[rows]
        qf_ref[0, h, HEAD_DIM:HEAD_DIM + 16, :] = ones3
        qf_ref[0, h, HEAD_DIM + 16:KPAD, :] = zeros_pad
        kaug = jnp.concatenate(
            [kf[rows]] + [t[h:h + 1] for t in nsplit]
            + [jnp.zeros((KPAD - HEAD_DIM - 3, TS), F32)], axis=0)
        kf_ref[0, h] = kaug.T.astype(BF16)
        vf_ref[0, h, 0:HEAD_DIM, :] = vf[rows]
        vf_ref[0, h, HEAD_DIM:VROWS, :] = ones1

    qs = (proj(wsq_ref, ub) * (0.5 * HEAD_DIM ** -0.5)).astype(BF16)
    ks = proj(wsk_ref, up)
    vs = proj(wsv_ref, up).astype(BF16)
    g = proj(wsg_ref, ub)
    gs_ref[0] = g * jax.nn.sigmoid(g)
    for h in range(N_HEADS):
        rows = slice(h * HEAD_DIM, (h + 1) * HEAD_DIM)
        qs_ref[0, h, 0:HEAD_DIM, :] = qs[rows]
        qs_ref[0, h, HEAD_DIM:KPAD, :] = zeros_half
        kpad = jnp.concatenate([ks[rows], jnp.zeros((KPAD - HEAD_DIM, TS), F32)], axis=0)
        ks_ref[0, h] = kpad.T.astype(BF16)
        vs_ref[0, h] = vs[rows]


def _proj(x, mod3, weights, b_f, tri, perm):
    B, S, D = x.shape
    full = lambda a: pl.BlockSpec(a.shape, lambda b, s: (0,) * a.ndim)
    head_spec = lambda r: pl.BlockSpec((1, N_HEADS, r, TS), lambda b, s: (b, 0, 0, s))
    flat_spec = pl.BlockSpec((1, WIDTH, TS), lambda b, s: (b, 0, s))
    head_shape = lambda r: jax.ShapeDtypeStruct((B, N_HEADS, r, S), BF16)
    flat_shape = jax.ShapeDtypeStruct((B, WIDTH, S), F32)
    krow_spec = pl.BlockSpec((1, N_HEADS, TS, KPAD), lambda b, s: (b, 0, s, 0))
    krow_shape = jax.ShapeDtypeStruct((B, N_HEADS, S, KPAD), BF16)
    return pl.pallas_call(
        _proj_kernel,
        out_shape=(head_shape(KPAD), krow_shape, head_shape(VROWS), flat_shape,
                   head_shape(KPAD), krow_shape, head_shape(HEAD_DIM), flat_shape),
        grid=(B, S // TS),
        in_specs=[pl.BlockSpec((1, TS, D), lambda b, s: (b, s, 0)),
                  pl.BlockSpec((1, 3, D), lambda b, s: (b, 0, 0))]
                 + [full(w) for w in weights] + [full(b_f), full(tri), full(perm)],
        out_specs=(head_spec(KPAD), krow_spec, head_spec(VROWS), flat_spec,
                   head_spec(KPAD), krow_spec, head_spec(HEAD_DIM), flat_spec),
        scratch_shapes=[pltpu.VMEM((N_HEADS, 128), F32)],
        compiler_params=pltpu.CompilerParams(
            dimension_semantics=("arbitrary", "arbitrary"),
            vmem_limit_bytes=56 * 1024 * 1024),
        name="proj",
    )(x, mod3, *weights, b_f, tri, perm)


def _suffix_prod_sublanes(t):
    r = lax.broadcasted_iota(jnp.int32, t.shape, 0)
    incl = t
    for d in (1, 2, 4):
        incl = incl * jnp.where(r + d < STRANDS, pltpu.roll(incl, STRANDS - d, axis=0), 1.0)
    excl = jnp.where(r + 1 < STRANDS, pltpu.roll(incl, STRANDS - 1, axis=0), 1.0)
    return excl, incl


def _sb_weights(z, keep_ref, carry, diagonal):
    keep = 0.5 - 0.5 * jnp.tanh(z)
    if diagonal:
        row = lax.broadcasted_iota(jnp.int32, (BK, BQ), 0)
        sub_row = row % SUB
        kpos = (row // SUB) * SUB + (sub_row % STRANDS) * STRAND_LEN + sub_row // STRANDS
        strict = kpos < lax.broadcasted_iota(jnp.int32, (BK, BQ), 1)
        keep = jnp.where(strict, keep, 1.0)
    keep_ref[...] = keep
    n_sub = BK // SUB
    rows = lambda sub, i: keep_ref[pl.ds(sub * SUB + i * STRANDS, STRANDS), :]
    ws = [[None] * STRAND_LEN for _ in range(n_sub)]
    for sub in reversed(range(n_sub)):
        total = rows(sub, 0)
        for i in range(1, STRAND_LEN):
            total = total * rows(sub, i)
        excl, incl = _suffix_prod_sublanes(total)
        run = carry * excl
        carry = carry * jnp.broadcast_to(incl[0:1], carry.shape)
        for i in reversed(range(STRAND_LEN)):
            nxt = run * rows(sub, i)
            ws[sub][i] = run - nxt
            run = nxt
    w = jnp.concatenate([x for sub_ws in ws for x in sub_ws], axis=0)
    return w.astype(BF16), carry


class _Item:
    def __init__(self, scores, weights, values):
        self.scores, self.weights, self.values = scores, weights, values


def _pipeline(items, finish_pending, prefetch, prefetch_step):
    n = len(items)
    items[0].scores()
    for step in range(1, n + 2):
        if step <= n:
            items[step - 1].weights()
        if step == 1:
            finish_pending()
        if step >= 2:
            items[step - 2].values()
        if step < n:
            items[step].scores()
        if step == prefetch_step:
            prefetch()


def _attn_tiles(refs, qs, tiles, ks_next, ks_pend, state, diagonal):
    qf_ref, kf_ref, vf_ref, qs_ref, ks_ref, vs_ref, s_ref, p_ref, keep_ref = refs
    cur = [None] * (4 * HP) if diagonal else list(state)
    fox_m = lambda h: 2 * h
    fox_acc = lambda h: 2 * h + 1
    sb_carry = lambda h: 2 * HP + 2 * h
    sb_acc = lambda h: 2 * HP + 2 * h + 1

    def finish_pending():
        if not diagonal:
            i = fox_acc(HP - 1)
            cur[i] = cur[i] + jnp.dot(vf_ref[0, HP - 1, :, pl.ds(ks_pend, BK)], p_ref[...],
                                      preferred_element_type=F32)

    def prefetch():
        s_ref[...] = jnp.dot(ks_ref[0, 0, pl.ds(ks_next, BK), :], qs_ref[0, 0, :, pl.ds(qs, BQ)],
                             preferred_element_type=F32)

    def fox_item(h, ti):
        ks = tiles[ti]
        deferred = h == HP - 1 and ti == len(tiles) - 1
        box = {}

        def scores():
            s = jnp.dot(kf_ref[0, h, pl.ds(ks, BK), :], qf_ref[0, h, :, pl.ds(qs, BQ)],
                        preferred_element_type=F32)
            if diagonal:
                kpos = lax.broadcasted_iota(jnp.int32, (BK, BQ), 0)
                qpos = lax.broadcasted_iota(jnp.int32, (BK, BQ), 1)
                s = jnp.where(kpos <= qpos, s, -jnp.inf)
            box["s"] = s

        def weights():
            if diagonal:
                cur[fox_m(h)] = jnp.max(box["s"], axis=0, keepdims=True)
            box["p"] = jnp.exp2(box["s"] - cur[fox_m(h)]).astype(BF16)

        def values():
            if deferred:
                p_ref[...] = box["p"]
                if diagonal:
                    cur[fox_acc(h)] = jnp.zeros((VROWS, BQ), F32)
                return
            pv = jnp.dot(vf_ref[0, h, :, pl.ds(ks, BK)], box["p"], preferred_element_type=F32)
            cur[fox_acc(h)] = pv if diagonal else cur[fox_acc(h)] + pv

        return _Item(scores, weights, values)

    def sb_item(h, ti):
        ks = tiles[ti]
        prefetched = h == 0 and ti == 0 and not diagonal
        box = {}

        def scores():
            if prefetched:
                box["z"] = s_ref[...]
            else:
                box["z"] = jnp.dot(ks_ref[0, h, pl.ds(ks, BK), :], qs_ref[0, h, :, pl.ds(qs, BQ)],
                                   preferred_element_type=F32)

        def weights():
            carry = jnp.ones((STRANDS, BQ), F32) if diagonal else cur[sb_carry(h)]
            box["w"], cur[sb_carry(h)] = _sb_weights(
                box["z"], keep_ref.at[(ti * HP + h) % 2], carry, diagonal)

        def values():
            pv = jnp.dot(vs_ref[0, h, :, pl.ds(ks, BK)], box["w"], preferred_element_type=F32)
            cur[sb_acc(h)] = pv if diagonal else cur[sb_acc(h)] + pv

        return _Item(scores, weights, values)

    items = []
    for ti in range(len(tiles)):
        for h in range(HP):
            items += [sb_item(h, ti), fox_item(h, ti)]
    _pipeline(items, finish_pending, prefetch, len(items) - 2)
    return tuple(cur)


def _fox_online(qf_ref, kf_ref, vf_ref, h, qs, qi):
    q = qf_ref[0, h, :, pl.ds(qs, BQ)]
    s = jnp.dot(kf_ref[0, h, pl.ds(qs, BK), :], q, preferred_element_type=F32)
    kpos = lax.broadcasted_iota(jnp.int32, (BK, BQ), 0)
    qpos = lax.broadcasted_iota(jnp.int32, (BK, BQ), 1)
    s = jnp.where(kpos <= qpos, s, -jnp.inf)
    m = jnp.max(s, axis=0, keepdims=True)
    acc = jnp.dot(vf_ref[0, h, :, pl.ds(qs, BK)], jnp.exp2(s - m).astype(BF16),
                  preferred_element_type=F32)

    def body(kj, mc):
        m, acc = mc
        ks = pl.multiple_of(kj * BK, BK)
        s = jnp.dot(kf_ref[0, h, pl.ds(ks, BK), :], q, preferred_element_type=F32)
        m_new = jnp.maximum(m, jnp.max(s, axis=0, keepdims=True))
        pv = jnp.dot(vf_ref[0, h, :, pl.ds(ks, BK)], jnp.exp2(s - m_new).astype(BF16),
                     preferred_element_type=F32)
        return m_new, jnp.exp2(m - m_new) * acc + pv

    return lax.fori_loop(0, qi, body, (m, acc))[1]


def _attn_kernel(qf_ref, kf_ref, vf_ref, qs_ref, ks_ref, vs_ref, of_ref, os_ref,
                 s_ref, p_ref, keep_ref, *, seq):
    refs = (qf_ref, kf_ref, vf_ref, qs_ref, ks_ref, vs_ref, s_ref, p_ref, keep_ref)

    def qbody(qi, carry):
        qs = pl.multiple_of(qi * BQ, BQ)
        tile = lambda j: pl.multiple_of(jnp.maximum(j, 0) * BK, BK)
        state = _attn_tiles(refs, qs, [qs], tile(qi - 1), None, None, True)

        def single(carry):
            st, ks_pend = carry
            return _attn_tiles(refs, qs, [tile(qi - 1)], tile(qi - 2),
                               pl.multiple_of(ks_pend, BK), st, False), tile(qi - 1)

        after_odd = lax.cond(qi % 2 == 1, single, lambda c: c, (state, qs))
        pairs = qi // 2

        def kbody(t, carry):
            st, ks_pend = carry
            j = 2 * (pairs - t) - 1
            return _attn_tiles(refs, qs, [tile(j), tile(j - 1)], tile(j - 2),
                               pl.multiple_of(ks_pend, BK), st, False), tile(j - 1)

        st, ks_pend = lax.fori_loop(0, pairs, kbody, after_odd)

        for h in range(HP):
            os_ref[0, h * HEAD_DIM:(h + 1) * HEAD_DIM, pl.ds(qs, BQ)] = st[2 * HP + 2 * h + 1]

        overflow = jnp.zeros((1, BQ), F32)
        for h in range(HP):
            acc = st[2 * h + 1]
            if h == HP - 1:
                acc = acc + jnp.dot(vf_ref[0, h, :, pl.ds(pl.multiple_of(ks_pend, BK), BK)],
                                    p_ref[...], preferred_element_type=F32)
            bad = jnp.where(jnp.abs(acc) < jnp.inf, 0.0, 1.0)
            overflow = jnp.maximum(overflow, jnp.max(bad, axis=0, keepdims=True))
            of_ref[0, h * HEAD_DIM:(h + 1) * HEAD_DIM, pl.ds(qs, BQ)] = (
                acc[0:HEAD_DIM] / acc[HEAD_DIM:HEAD_DIM + 1])

        @pl.when(jnp.max(overflow) > 0.0)
        def _():
            for h in range(HP):
                acc = _fox_online(qf_ref, kf_ref, vf_ref, h, qs, qi)
                of_ref[0, h * HEAD_DIM:(h + 1) * HEAD_DIM, pl.ds(qs, BQ)] = (
                    acc[0:HEAD_DIM] / acc[HEAD_DIM:HEAD_DIM + 1])

        return carry

    lax.fori_loop(0, seq // BQ, qbody, 0)


def _attention(qf, kf, vf, qs, ks, vs):
    B, H, _, S = qf.shape
    once = pl.Buffered(1)
    qspec = pl.BlockSpec((1, HP, KPAD, S), lambda b, g: (b, g, 0, 0), pipeline_mode=once)
    kspec = pl.BlockSpec((1, HP, S, KPAD), lambda b, g: (b, g, 0, 0), pipeline_mode=once)
    vspec = lambda r: pl.BlockSpec((1, HP, r, S), lambda b, g: (b, g, 0, 0), pipeline_mode=once)
    ospec = pl.BlockSpec((1, HP * HEAD_DIM, S), lambda b, g: (b, g, 0))
    oshape = jax.ShapeDtypeStruct((B, H * HEAD_DIM, S), F32)
    return pl.pallas_call(
        functools.partial(_attn_kernel, seq=S),
        out_shape=(oshape, oshape),
        grid=(B, H // HP),
        in_specs=[qspec, kspec, vspec(VROWS), qspec, kspec, vspec(HEAD_DIM)],
        out_specs=(ospec, ospec),
        scratch_shapes=[pltpu.VMEM((BK, BQ), F32), pltpu.VMEM((BK, BQ), BF16),
                        pltpu.VMEM((HP, BK, BQ), F32)],
        compiler_params=pltpu.CompilerParams(
            dimension_semantics=("arbitrary", "arbitrary"),
            vmem_limit_bytes=56 * 1024 * 1024),
        name="attn",
    )(qf, kf, vf, qs, ks, vs)


def _out_kernel(yf_ref, ys_ref, gf_ref, gs_ref, x_ref, mod_ref, w_ref, lng_ref, lnb_ref, o_ref):
    yg = jnp.concatenate([yf_ref[0] * gf_ref[0], ys_ref[0] * gs_ref[0]], axis=0)
    y = jnp.dot(yg.T.astype(BF16), w_ref[...], preferred_element_type=F32)
    gate = mod_ref[0, 2:3, :]
    resid = DEEPNORM_ALPHA * x_ref[0] + gate * y
    mu = jnp.mean(resid, axis=-1, keepdims=True)
    rc = resid - mu
    var = jnp.mean(rc * rc, axis=-1, keepdims=True)
    o_ref[0] = rc * lax.rsqrt(var + LN_EPS) * lng_ref[...] + lnb_ref[...]


def _out(yf, ys, gf, gs, x, mod3, w_out, ln_g, ln_b):
    B, S, D = x.shape
    flat_spec = pl.BlockSpec((1, WIDTH, TS), lambda b, s: (b, 0, s))
    full = lambda a: pl.BlockSpec(a.shape, lambda b, s: (0,) * a.ndim)
    return pl.pallas_call(
        _out_kernel,
        out_shape=jax.ShapeDtypeStruct((B, S, D), F32),
        grid=(B, S // TS),
        in_specs=[flat_spec, flat_spec, flat_spec, flat_spec,
                  pl.BlockSpec((1, TS, D), lambda b, s: (b, s, 0)),
                  pl.BlockSpec((1, 3, D), lambda b, s: (b, 0, 0)),
                  full(w_out), full(ln_g), full(ln_b)],
        out_specs=pl.BlockSpec((1, TS, D), lambda b, s: (b, s, 0)),
        compiler_params=pltpu.CompilerParams(
            dimension_semantics=("arbitrary", "arbitrary"),
            vmem_limit_bytes=48 * 1024 * 1024),
        name="out",
    )(yf, ys, gf, gs, x, mod3, w_out, ln_g, ln_b)


def kernel(x, c, w_ada, b_ada, w_in, b_f, w_out, ln_g, ln_b):
    B, S, D = x.shape
    assert D == D_MODEL and S % TS == 0 and w_ada.shape[0] == DEPTH
    tri = (jnp.arange(TS)[:, None] <= jnp.arange(TS)[None, :]).astype(BF16)
    p = jnp.arange(TS)
    src = (p // SUB) * SUB + (p % STRANDS) * STRAND_LEN + (p % SUB) // STRANDS
    perm = (src[:, None] == jnp.arange(TS)[None, :]).astype(BF16)
    for layer in range(DEPTH):
        c8 = jnp.pad(c, ((0, 8 - B), (0, 0)))
        mod = _ada(c8, w_ada[layer], b_ada[layer][None, :])[:B]
        mod3 = mod.reshape(B, 3, D)
        wt = w_in[layer].T.astype(BF16)
        o = 0
        parts = []
        for wdt in (WIDTH, WIDTH, WIDTH, WIDTH, N_HEADS, WIDTH, WIDTH, WIDTH, WIDTH):
            parts.append(wt[o:o + wdt])
            o += wdt
        wfq, wfk, wfv, wfg, wff, wsq, wsk, wsv, wsg = parts
        wff = jnp.pad(wff, ((0, 16 - N_HEADS), (0, 0)))
        weights = (wfq, wfk, wfv, wfg, wff, wsq, wsk, wsv, wsg)
        qf, kf, vf, gf, qs, ks, vs, gs = _proj(x, mod3, weights, b_f[layer][:, None], tri, perm)
        yf, ys = _attention(qf, kf, vf, qs, ks, vs)
        x = _out(yf, ys, gf, gs, x, mod3, w_out[layer].astype(BF16),
                 ln_g[layer][None, :], ln_b[layer][None, :])
    return x
```

```python
import functools

import jax
import jax.numpy as jnp
from jax import lax
from jax.experimental import pallas as pl
from jax.experimental.pallas import tpu as pltpu

D_MODEL = 1024
HEAD_DIM = 64
N_HEADS = 8
WIDTH = N_HEADS * HEAD_DIM
LN_EPS = 1e-5
DEPTH = 1
DEEPNORM_ALPHA = (2.0 * DEPTH) ** 0.25

TS = 512
BQ = 512
BK = 512
HP = 2
TILES_PER_BODY = 4
SUB = 256
STRANDS = 8
STRAND_LEN = SUB // STRANDS
KPAD = 128
VROWS = 80
LOG2E = 1.4426950408889634

F32 = jnp.float32
BF16 = jnp.bfloat16
NT_DIMS = (((1,), (1,)), ((), ()))


def _split3(f):
    hi = f.astype(BF16)
    r1 = f - hi.astype(F32)
    mid = r1.astype(BF16)
    lo = (r1 - mid.astype(F32)).astype(BF16)
    return hi, mid, lo


def _ada_kernel(c_ref, w_ref, b_ref, o_ref):
    c = c_ref[...]
    a = c * jax.nn.sigmoid(c)
    o_ref[...] = jnp.dot(a, w_ref[...], preferred_element_type=F32,
                         precision=lax.Precision.HIGHEST) + b_ref[...]


def _ada(c8, w_ada, b_ada):
    n = w_ada.shape[1]
    return pl.pallas_call(
        _ada_kernel,
        out_shape=jax.ShapeDtypeStruct((8, n), F32),
        grid=(n // D_MODEL,),
        in_specs=[pl.BlockSpec((8, D_MODEL), lambda j: (0, 0)),
                  pl.BlockSpec((D_MODEL, D_MODEL), lambda j: (0, j)),
                  pl.BlockSpec((1, D_MODEL), lambda j: (0, j))],
        out_specs=pl.BlockSpec((8, D_MODEL), lambda j: (0, j)),
        name="ada",
    )(c8, w_ada, b_ada)


def _proj_kernel(x_ref, mod_ref, wfq_ref, wfk_ref, wfv_ref, wfg_ref, wff_ref,
                 wsq_ref, wsk_ref, wsv_ref, wsg_ref, bf_ref, tri_ref, perm_ref,
                 qf_ref, kf_ref, vf_ref, gf_ref, qs_ref, ks_ref, vs_ref, gs_ref,
                 carry_ref):
    @pl.when(pl.program_id(1) == 0)
    def _():
        carry_ref[...] = jnp.zeros_like(carry_ref)

    x = x_ref[0]
    mu = jnp.mean(x, axis=-1, keepdims=True)
    xc = x - mu
    var = jnp.mean(xc * xc, axis=-1, keepdims=True)
    shift = mod_ref[0, 0:1, :]
    scale = mod_ref[0, 1:2, :]
    u = xc * lax.rsqrt(var + LN_EPS) * (1.0 + scale) + shift
    ub = u.astype(BF16)
    up = jnp.dot(perm_ref[...], ub, preferred_element_type=F32).astype(BF16)

    def proj(w_ref, lhs):
        return lax.dot_general(w_ref[...], lhs, NT_DIMS, preferred_element_type=F32)

    ff = proj(wff_ref, ub)[0:N_HEADS]
    logf = jax.nn.log_sigmoid(ff + bf_ref[...])
    local = jnp.zeros((N_HEADS, TS), F32)
    for part in _split3(logf):
        local = local + jnp.dot(part, tri_ref[...], preferred_element_type=F32)
    fcum = local + carry_ref[:, 0:1]
    carry_ref[...] = jnp.broadcast_to(fcum[:, TS - 1:TS], carry_ref.shape)
    nsplit = [t.astype(F32) for t in _split3(-LOG2E * fcum)]

    zeros_pad = jnp.zeros((KPAD - HEAD_DIM - 16, TS), BF16)
    zeros_half = jnp.zeros((KPAD - HEAD_DIM, TS), BF16)
    ones3 = (lax.broadcasted_iota(jnp.int32, (16, TS), 0) < 3).astype(BF16)
    ones1 = (lax.broadcasted_iota(jnp.int32, (16, TS), 0) < 1).astype(BF16)

    qf = (proj(wfq_ref, ub) * (LOG2E * HEAD_DIM ** -0.5)).astype(BF16)
    kf = proj(wfk_ref, ub)
    vf = proj(wfv_ref, ub).astype(BF16)
    g = proj(wfg_ref, ub)
    gf_ref[0] = (g * jax.nn.sigmoid(g)).astype(gf_ref.dtype)
    for h in range(N_HEADS):
        rows = slice(h * HEAD_DIM, (h + 1) * HEAD_DIM)
        qf_ref[0, h, 0:HEAD_DIM, :] = qf[rows]
        qf_ref[0, h, HEAD_DIM:HEAD_DIM + 16, :] = ones3
        qf_ref[0, h, HEAD_DIM + 16:KPAD, :] = zeros_pad
        kaug = jnp.concatenate(
            [kf[rows]] + [t[h:h + 1] for t in nsplit]
            + [jnp.zeros((KPAD - HEAD_DIM - 3, TS), F32)], axis=0)
        kf_ref[0, h] = kaug.T.astype(BF16)
        vf_ref[0, h, 0:HEAD_DIM, :] = vf[rows]
        vf_ref[0, h, HEAD_DIM:VROWS, :] = ones1

    qs = (proj(wsq_ref, ub) * (0.5 * HEAD_DIM ** -0.5)).astype(BF16)
    ks = proj(wsk_ref, up)
    vs = proj(wsv_ref, up).astype(BF16)
    g = proj(wsg_ref, ub)
    gs_ref[0] = (g * jax.nn.sigmoid(g)).astype(gs_ref.dtype)
    for h in range(N_HEADS):
        rows = slice(h * HEAD_DIM, (h + 1) * HEAD_DIM)
        qs_ref[0, h, 0:HEAD_DIM, :] = qs[rows]
        qs_ref[0, h, HEAD_DIM:KPAD, :] = zeros_half
        kpad = jnp.concatenate([ks[rows], jnp.zeros((KPAD - HEAD_DIM, TS), F32)], axis=0)
        ks_ref[0, h] = kpad.T.astype(BF16)
        vs_ref[0, h] = vs[rows]


def _proj(x, mod3, weights, b_f, tri, perm):
    B, S, D = x.shape
    full = lambda a: pl.BlockSpec(a.shape, lambda b, s: (0,) * a.ndim)
    head_spec = lambda r: pl.BlockSpec((1, N_HEADS, r, TS), lambda b, s: (b, 0, 0, s))
    flat_spec = pl.BlockSpec((1, WIDTH, TS), lambda b, s: (b, 0, s))
    head_shape = lambda r: jax.ShapeDtypeStruct((B, N_HEADS, r, S), BF16)
    flat_shape = jax.ShapeDtypeStruct((B, WIDTH, S), BF16)
    krow_spec = pl.BlockSpec((1, N_HEADS, TS, KPAD), lambda b, s: (b, 0, s, 0))
    krow_shape = jax.ShapeDtypeStruct((B, N_HEADS, S, KPAD), BF16)
    return pl.pallas_call(
        _proj_kernel,
        out_shape=(head_shape(KPAD), krow_shape, head_shape(VROWS), flat_shape,
                   head_shape(KPAD), krow_shape, head_shape(HEAD_DIM), flat_shape),
        grid=(B, S // TS),
        in_specs=[pl.BlockSpec((1, TS, D), lambda b, s: (b, s, 0)),
                  pl.BlockSpec((1, 3, D), lambda b, s: (b, 0, 0))]
                 + [full(w) for w in weights] + [full(b_f), full(tri), full(perm)],
        out_specs=(head_spec(KPAD), krow_spec, head_spec(VROWS), flat_spec,
                   head_spec(KPAD), krow_spec, head_spec(HEAD_DIM), flat_spec),
        scratch_shapes=[pltpu.VMEM((N_HEADS, 128), F32)],
        compiler_params=pltpu.CompilerParams(
            dimension_semantics=("arbitrary", "arbitrary"),
            vmem_limit_bytes=56 * 1024 * 1024),
        name="proj",
    )(x, mod3, *weights, b_f, tri, perm)


def _suffix_prod_sublanes(t):
    r = lax.broadcasted_iota(jnp.int32, t.shape, 0)
    incl = t
    for d in (1, 2, 4):
        incl = incl * jnp.where(r + d < STRANDS, pltpu.roll(incl, STRANDS - d, axis=0), 1.0)
    excl = jnp.where(r + 1 < STRANDS, pltpu.roll(incl, STRANDS - 1, axis=0), 1.0)
    return excl, incl


def _sb_weights(z, keep_ref, carry, diagonal):
    keep = 0.5 - 0.5 * jnp.tanh(z)
    if diagonal:
        row = lax.broadcasted_iota(jnp.int32, (BK, BQ), 0)
        sub_row = row % SUB
        kpos = (row // SUB) * SUB + (sub_row % STRANDS) * STRAND_LEN + sub_row // STRANDS
        strict = kpos < lax.broadcasted_iota(jnp.int32, (BK, BQ), 1)
        keep = jnp.where(strict, keep, 1.0)
    keep_ref[...] = keep
    n_sub = BK // SUB
    rows = lambda sub, i: keep_ref[pl.ds(sub * SUB + i * STRANDS, STRANDS), :]
    ws = [[None] * STRAND_LEN for _ in range(n_sub)]
    for sub in reversed(range(n_sub)):
        total = rows(sub, 0)
        for i in range(1, STRAND_LEN):
            total = total * rows(sub, i)
        excl, incl = _suffix_prod_sublanes(total)
        run = carry * excl
        carry = carry * jnp.broadcast_to(incl[0:1], carry.shape)
        for i in reversed(range(STRAND_LEN)):
            nxt = run * rows(sub, i)
            ws[sub][i] = run - nxt
            run = nxt
    w = jnp.concatenate([x for sub_ws in ws for x in sub_ws], axis=0)
    return w.astype(BF16), carry


class _Item:
    def __init__(self, scores, weights, values):
        self.scores, self.weights, self.values = scores, weights, values


def _pipeline(items, finish_pending, prefetch, prefetch_step):
    n = len(items)
    items[0].scores()
    for step in range(1, n + 2):
        if step <= n:
            items[step - 1].weights()
        if step == 1:
            finish_pending()
        if step >= 2:
            items[step - 2].values()
        if step < n:
            items[step].scores()
        if step == prefetch_step:
            prefetch()


def _attn_tiles(refs, qs, tiles, ks_next, ks_pend, state, diagonal):
    qf_ref, kf_ref, vf_ref, qs_ref, ks_ref, vs_ref, s_ref, p_ref, keep_ref = refs
    cur = [None] * (4 * HP) if diagonal else list(state)
    fox_m = lambda h: 2 * h
    fox_acc = lambda h: 2 * h + 1
    sb_carry = lambda h: 2 * HP + 2 * h
    sb_acc = lambda h: 2 * HP + 2 * h + 1

    def finish_pending():
        if not diagonal:
            i = fox_acc(HP - 1)
            cur[i] = cur[i] + jnp.dot(vf_ref[0, HP - 1, :, pl.ds(ks_pend, BK)], p_ref[...],
                                      preferred_element_type=F32)

    def prefetch():
        s_ref[...] = jnp.dot(ks_ref[0, 0, pl.ds(ks_next, BK), :], qs_ref[0, 0, :, pl.ds(qs, BQ)],
                             preferred_element_type=F32)

    def fox_item(h, ti):
        ks = tiles[ti]
        deferred = h == HP - 1 and ti == len(tiles) - 1
        box = {}

        def scores():
            s = jnp.dot(kf_ref[0, h, pl.ds(ks, BK), :], qf_ref[0, h, :, pl.ds(qs, BQ)],
                        preferred_element_type=F32)
            if diagonal:
                kpos = lax.broadcasted_iota(jnp.int32, (BK, BQ), 0)
                qpos = lax.broadcasted_iota(jnp.int32, (BK, BQ), 1)
                s = jnp.where(kpos <= qpos, s, -jnp.inf)
            box["s"] = s

        def weights():
            if diagonal:
                cur[fox_m(h)] = jnp.max(box["s"], axis=0, keepdims=True)
            box["p"] = jnp.exp2(box["s"] - cur[fox_m(h)]).astype(BF16)

        def values():
            if deferred:
                p_ref[...] = box["p"]
                if diagonal:
                    cur[fox_acc(h)] = jnp.zeros((VROWS, BQ), F32)
                return
            pv = jnp.dot(vf_ref[0, h, :, pl.ds(ks, BK)], box["p"], preferred_element_type=F32)
            cur[fox_acc(h)] = pv if diagonal else cur[fox_acc(h)] + pv

        return _Item(scores, weights, values)

    def sb_item(h, ti):
        ks = tiles[ti]
        prefetched = h == 0 and ti == 0 and not diagonal
        box = {}

        def scores():
            if prefetched:
                box["z"] = s_ref[...]
            else:
                box["z"] = jnp.dot(ks_ref[0, h, pl.ds(ks, BK), :], qs_ref[0, h, :, pl.ds(qs, BQ)],
                                   preferred_element_type=F32)

        def weights():
            carry = jnp.ones((STRANDS, BQ), F32) if diagonal else cur[sb_carry(h)]
            box["w"], cur[sb_carry(h)] = _sb_weights(
                box["z"], keep_ref.at[(ti * HP + h) % 2], carry, diagonal)

        def values():
            pv = jnp.dot(vs_ref[0, h, :, pl.ds(ks, BK)], box["w"], preferred_element_type=F32)
            cur[sb_acc(h)] = pv if diagonal else cur[sb_acc(h)] + pv

        return _Item(scores, weights, values)

    items = []
    for ti in range(len(tiles)):
        for h in range(HP):
            items += [sb_item(h, ti), fox_item(h, ti)]
    _pipeline(items, finish_pending, prefetch, len(items) - 2)
    return tuple(cur)


def _fox_online(qf_ref, kf_ref, vf_ref, h, qs, qi):
    q = qf_ref[0, h, :, pl.ds(qs, BQ)]
    s = jnp.dot(kf_ref[0, h, pl.ds(qs, BK), :], q, preferred_element_type=F32)
    kpos = lax.broadcasted_iota(jnp.int32, (BK, BQ), 0)
    qpos = lax.broadcasted_iota(jnp.int32, (BK, BQ), 1)
    s = jnp.where(kpos <= qpos, s, -jnp.inf)
    m = jnp.max(s, axis=0, keepdims=True)
    acc = jnp.dot(vf_ref[0, h, :, pl.ds(qs, BK)], jnp.exp2(s - m).astype(BF16),
                  preferred_element_type=F32)

    def body(kj, mc):
        m, acc = mc
        ks = pl.multiple_of(kj * BK, BK)
        s = jnp.dot(kf_ref[0, h, pl.ds(ks, BK), :], q, preferred_element_type=F32)
        m_new = jnp.maximum(m, jnp.max(s, axis=0, keepdims=True))
        pv = jnp.dot(vf_ref[0, h, :, pl.ds(ks, BK)], jnp.exp2(s - m_new).astype(BF16),
                     preferred_element_type=F32)
        return m_new, jnp.exp2(m - m_new) * acc + pv

    return lax.fori_loop(0, qi, body, (m, acc))[1]


def _attn_kernel(qf_ref, kf_ref, vf_ref, qs_ref, ks_ref, vs_ref, of_ref, os_ref,
                 s_ref, p_ref, keep_ref, *, seq):
    refs = (qf_ref, kf_ref, vf_ref, qs_ref, ks_ref, vs_ref, s_ref, p_ref, keep_ref)

    def qbody(qi, carry):
        qs = pl.multiple_of(qi * BQ, BQ)
        tile = lambda j: pl.multiple_of(jnp.maximum(j, 0) * BK, BK)
        state = _attn_tiles(refs, qs, [qs], tile(qi - 1), None, None, True)

        def run(n, top, carry):
            st, ks_pend = carry
            tiles = [tile(top - i) for i in range(n)]
            return _attn_tiles(refs, qs, tiles, tile(top - n), pl.multiple_of(ks_pend, BK),
                               st, False), tiles[-1]

        carry_k, done, n = (state, qs), 0, 1
        while n < TILES_PER_BODY:
            chunk = qi & n
            carry_k = lax.cond(chunk != 0, functools.partial(run, n, qi - 1 - done),
                               lambda c: c, carry_k)
            done, n = done + chunk, 2 * n
        bodies = (qi - done) // TILES_PER_BODY

        def kbody(t, carry):
            return run(TILES_PER_BODY, qi - 1 - done - TILES_PER_BODY * t, carry)

        st, ks_pend = lax.fori_loop(0, bodies, kbody, carry_k)

        for h in range(HP):
            os_ref[0, h * HEAD_DIM:(h + 1) * HEAD_DIM, pl.ds(qs, BQ)] = (
                st[2 * HP + 2 * h + 1].astype(os_ref.dtype))

        overflow = jnp.zeros((1, BQ), F32)
        for h in range(HP):
            acc = st[2 * h + 1]
            if h == HP - 1:
                acc = acc + jnp.dot(vf_ref[0, h, :, pl.ds(pl.multiple_of(ks_pend, BK), BK)],
                                    p_ref[...], preferred_element_type=F32)
            bad = jnp.where(jnp.abs(acc) < jnp.inf, 0.0, 1.0)
            overflow = jnp.maximum(overflow, jnp.max(bad, axis=0, keepdims=True))
            of_ref[0, h * HEAD_DIM:(h + 1) * HEAD_DIM, pl.ds(qs, BQ)] = (
                acc[0:HEAD_DIM] / acc[HEAD_DIM:HEAD_DIM + 1]).astype(of_ref.dtype)

        @pl.when(jnp.max(overflow) > 0.0)
        def _():
            for h in range(HP):
                acc = _fox_online(qf_ref, kf_ref, vf_ref, h, qs, qi)
                of_ref[0, h * HEAD_DIM:(h + 1) * HEAD_DIM, pl.ds(qs, BQ)] = (
                    acc[0:HEAD_DIM] / acc[HEAD_DIM:HEAD_DIM + 1]).astype(of_ref.dtype).astype(of_ref.dtype)

        return carry

    lax.fori_loop(0, seq // BQ, qbody, 0)


def _attention(qf, kf, vf, qs, ks, vs):
    B, H, _, S = qf.shape
    once = pl.Buffered(1)
    qspec = pl.BlockSpec((1, HP, KPAD, S), lambda b, g: (b, g, 0, 0), pipeline_mode=once)
    kspec = pl.BlockSpec((1, HP, S, KPAD), lambda b, g: (b, g, 0, 0), pipeline_mode=once)
    vspec = lambda r: pl.BlockSpec((1, HP, r, S), lambda b, g: (b, g, 0, 0), pipeline_mode=once)
    ospec = pl.BlockSpec((1, HP * HEAD_DIM, S), lambda b, g: (b, g, 0))
    oshape = jax.ShapeDtypeStruct((B, H * HEAD_DIM, S), BF16)
    return pl.pallas_call(
        functools.partial(_attn_kernel, seq=S),
        out_shape=(oshape, oshape),
        grid=(B, H // HP),
        in_specs=[qspec, kspec, vspec(VROWS), qspec, kspec, vspec(HEAD_DIM)],
        out_specs=(ospec, ospec),
        scratch_shapes=[pltpu.VMEM((BK, BQ), F32), pltpu.VMEM((BK, BQ), BF16),
                        pltpu.VMEM((HP, BK, BQ), F32)],
        compiler_params=pltpu.CompilerParams(
            dimension_semantics=("arbitrary", "arbitrary"),
            vmem_limit_bytes=56 * 1024 * 1024),
        name="attn",
    )(qf, kf, vf, qs, ks, vs)


def _out_kernel(yf_ref, ys_ref, gf_ref, gs_ref, x_ref, mod_ref, w_ref, lng_ref, lnb_ref, o_ref):
    gated = lambda y_ref, g_ref: y_ref[0].astype(F32) * g_ref[0].astype(F32)
    yg = jnp.concatenate([gated(yf_ref, gf_ref), gated(ys_ref, gs_ref)], axis=0)
    y = jnp.dot(yg.T.astype(BF16), w_ref[...], preferred_element_type=F32)
    gate = mod_ref[0, 2:3, :]
    resid = DEEPNORM_ALPHA * x_ref[0] + gate * y
    mu = jnp.mean(resid, axis=-1, keepdims=True)
    rc = resid - mu
    var = jnp.mean(rc * rc, axis=-1, keepdims=True)
    o_ref[0] = rc * lax.rsqrt(var + LN_EPS) * lng_ref[...] + lnb_ref[...]


def _out(yf, ys, gf, gs, x, mod3, w_out, ln_g, ln_b):
    B, S, D = x.shape
    flat_spec = pl.BlockSpec((1, WIDTH, TS), lambda b, s: (b, 0, s))
    full = lambda a: pl.BlockSpec(a.shape, lambda b, s: (0,) * a.ndim)
    return pl.pallas_call(
        _out_kernel,
        out_shape=jax.ShapeDtypeStruct((B, S, D), F32),
        grid=(B, S // TS),
        in_specs=[flat_spec, flat_spec, flat_spec, flat_spec,
                  pl.BlockSpec((1, TS, D), lambda b, s: (b, s, 0)),
                  pl.BlockSpec((1, 3, D), lambda b, s: (b, 0, 0)),
                  full(w_out), full(ln_g), full(ln_b)],
        out_specs=pl.BlockSpec((1, TS, D), lambda b, s: (b, s, 0)),
        compiler_params=pltpu.CompilerParams(
            dimension_semantics=("arbitrary", "arbitrary"),
            vmem_limit_bytes=48 * 1024 * 1024),
        name="out",
    )(yf, ys, gf, gs, x, mod3, w_out, ln_g, ln_b)


def kernel(x, c, w_ada, b_ada, w_in, b_f, w_out, ln_g, ln_b):
    B, S, D = x.shape
    assert D == D_MODEL and S % TS == 0 and w_ada.shape[0] == DEPTH
    tri = (jnp.arange(TS)[:, None] <= jnp.arange(TS)[None, :]).astype(BF16)
    p = jnp.arange(TS)
    src = (p // SUB) * SUB + (p % STRANDS) * STRAND_LEN + (p % SUB) // STRANDS
    perm = (src[:, None] == jnp.arange(TS)[None, :]).astype(BF16)
    for layer in range(DEPTH):
        c8 = jnp.pad(c, ((0, 8 - B), (0, 0)))
        mod = _ada(c8, w_ada[layer], b_ada[layer][None, :])[:B]
        mod3 = mod.reshape(B, 3, D)
        wt = w_in[layer].T.astype(BF16)
        o = 0
        parts = []
        for wdt in (WIDTH, WIDTH, WIDTH, WIDTH, N_HEADS, WIDTH, WIDTH, WIDTH, WIDTH):
            parts.append(wt[o:o + wdt])
            o += wdt
        wfq, wfk, wfv, wfg, wff, wsq, wsk, wsv, wsg = parts
        wff = jnp.pad(wff, ((0, 16 - N_HEADS), (0, 0)))
        weights = (wfq, wfk, wfv, wfg, wff, wsq, wsk, wsv, wsg)
        qf, kf, vf, gf, qs, ks, vs, gs = _proj(x, mod3, weights, b_f[layer][:, None], tri, perm)
        yf, ys = _attention(qf, kf, vf, qs, ks, vs)
        x = _out(yf, ys, gf, gs, x, mod3, w_out[layer].astype(BF16),
                 ln_g[layer][None, :], ln_b[layer][None, :])
    return x
```

```python
import functools

import jax
import jax.numpy as jnp
from jax import lax
from jax.experimental import pallas as pl
from jax.experimental.pallas import tpu as pltpu

D_MODEL = 1024
HEAD_DIM = 64
N_HEADS = 8
WIDTH = N_HEADS * HEAD_DIM
LN_EPS = 1e-5
DEPTH = 1
DEEPNORM_ALPHA = (2.0 * DEPTH) ** 0.25

TS = 512
BQ = 512
BK = 512
HP = 2
TILES_PER_BODY = 4
SUB = 256
STRANDS = 8
STRAND_LEN = SUB // STRANDS
KPAD = 128
VROWS = 80
LOG2E = 1.4426950408889634

F32 = jnp.float32
BF16 = jnp.bfloat16
NT_DIMS = (((1,), (1,)), ((), ()))


def _split3(f):
    hi = f.astype(BF16)
    r1 = f - hi.astype(F32)
    mid = r1.astype(BF16)
    lo = (r1 - mid.astype(F32)).astype(BF16)
    return hi, mid, lo


def _ada_kernel(c_ref, w_ref, b_ref, o_ref):
    c = c_ref[...]
    a = c * jax.nn.sigmoid(c)
    o_ref[...] = jnp.dot(a, w_ref[...], preferred_element_type=F32,
                         precision=lax.Precision.HIGHEST) + b_ref[...]


def _ada(c8, w_ada, b_ada):
    n = w_ada.shape[1]
    return pl.pallas_call(
        _ada_kernel,
        out_shape=jax.ShapeDtypeStruct((8, n), F32),
        grid=(n // D_MODEL,),
        in_specs=[pl.BlockSpec((8, D_MODEL), lambda j: (0, 0)),
                  pl.BlockSpec((D_MODEL, D_MODEL), lambda j: (0, j)),
                  pl.BlockSpec((1, D_MODEL), lambda j: (0, j))],
        out_specs=pl.BlockSpec((8, D_MODEL), lambda j: (0, j)),
        name="ada",
    )(c8, w_ada, b_ada)


def _proj_kernel(x_ref, mod_ref, wfq_ref, wfk_ref, wfv_ref, wfg_ref, wff_ref,
                 wsq_ref, wsk_ref, wsv_ref, wsg_ref, bf_ref, tri_ref, perm_ref,
                 qf_ref, kf_ref, vf_ref, gf_ref, qs_ref, ks_ref, vs_ref, gs_ref,
                 carry_ref):
    @pl.when(pl.program_id(1) == 0)
    def _():
        carry_ref[...] = jnp.zeros_like(carry_ref)

    x = x_ref[0]
    mu = jnp.mean(x, axis=-1, keepdims=True)
    xc = x - mu
    var = jnp.mean(xc * xc, axis=-1, keepdims=True)
    shift = mod_ref[0, 0:1, :]
    scale = mod_ref[0, 1:2, :]
    u = xc * lax.rsqrt(var + LN_EPS) * (1.0 + scale) + shift
    ub = u.astype(BF16)
    up = jnp.dot(perm_ref[...], ub, preferred_element_type=F32).astype(BF16)

    def proj(w_ref, lhs):
        return lax.dot_general(w_ref[...], lhs, NT_DIMS, preferred_element_type=F32)

    ff = proj(wff_ref, ub)[0:N_HEADS]
    logf = jax.nn.log_sigmoid(ff + bf_ref[...])
    local = jnp.zeros((N_HEADS, TS), F32)
    for part in _split3(logf):
        local = local + jnp.dot(part, tri_ref[...], preferred_element_type=F32)
    fcum = local + carry_ref[:, 0:1]
    carry_ref[...] = jnp.broadcast_to(fcum[:, TS - 1:TS], carry_ref.shape)
    nsplit = [t.astype(F32) for t in _split3(-LOG2E * fcum)]

    zeros_pad = jnp.zeros((KPAD - HEAD_DIM - 16, TS), BF16)
    zeros_half = jnp.zeros((KPAD - HEAD_DIM, TS), BF16)
    ones3 = (lax.broadcasted_iota(jnp.int32, (16, TS), 0) < 3).astype(BF16)
    ones1 = (lax.broadcasted_iota(jnp.int32, (16, TS), 0) < 1).astype(BF16)

    qf = (proj(wfq_ref, ub) * (LOG2E * HEAD_DIM ** -0.5)).astype(BF16)
    kf = proj(wfk_ref, ub)
    vf = proj(wfv_ref, ub).astype(BF16)
    g = proj(wfg_ref, ub)
    gf_ref[0] = (g * jax.nn.sigmoid(g)).astype(gf_ref.dtype)
    for h in range(N_HEADS):
        rows = slice(h * HEAD_DIM, (h + 1) * HEAD_DIM)
        qf_ref[0, h, 0:HEAD_DIM, :] = qf[rows]
        qf_ref[0, h, HEAD_DIM:HEAD_DIM + 16, :] = ones3
        qf_ref[0, h, HEAD_DIM + 16:KPAD, :] = zeros_pad
        kaug = jnp.concatenate(
            [kf[rows]] + [t[h:h + 1] for t in nsplit]
            + [jnp.zeros((KPAD - HEAD_DIM - 3, TS), F32)], axis=0)
        kf_ref[0, h] = kaug.T.astype(BF16)
        vf_ref[0, h, 0:HEAD_DIM, :] = vf[rows]
        vf_ref[0, h, HEAD_DIM:VROWS, :] = ones1

    qs = (proj(wsq_ref, ub) * (0.5 * HEAD_DIM ** -0.5)).astype(BF16)
    ks = proj(wsk_ref, up)
    vs = proj(wsv_ref, up).astype(BF16)
    g = proj(wsg_ref, ub)
    gs_ref[0] = (g * jax.nn.sigmoid(g)).astype(gs_ref.dtype)
    for h in range(N_HEADS):
        rows = slice(h * HEAD_DIM, (h + 1) * HEAD_DIM)
        qs_ref[0, h, 0:HEAD_DIM, :] = qs[rows]
        qs_ref[0, h, HEAD_DIM:KPAD, :] = zeros_half
        kpad = jnp.concatenate([ks[rows], jnp.zeros((KPAD - HEAD_DIM, TS), F32)], axis=0)
        ks_ref[0, h] = kpad.T.astype(BF16)
        vs_ref[0, h] = vs[rows]


def _proj(x, mod3, weights, b_f, tri, perm):
    B, S, D = x.shape
    full = lambda a: pl.BlockSpec(a.shape, lambda b, s: (0,) * a.ndim)
    head_spec = lambda r: pl.BlockSpec((1, N_HEADS, r, TS), lambda b, s: (b, 0, 0, s))
    flat_spec = pl.BlockSpec((1, WIDTH, TS), lambda b, s: (b, 0, s))
    head_shape = lambda r: jax.ShapeDtypeStruct((B, N_HEADS, r, S), BF16)
    flat_shape = jax.ShapeDtypeStruct((B, WIDTH, S), BF16)
    krow_spec = pl.BlockSpec((1, N_HEADS, TS, KPAD), lambda b, s: (b, 0, s, 0))
    krow_shape = jax.ShapeDtypeStruct((B, N_HEADS, S, KPAD), BF16)
    return pl.pallas_call(
        _proj_kernel,
        out_shape=(head_shape(KPAD), krow_shape, head_shape(VROWS), flat_shape,
                   head_shape(KPAD), krow_shape, head_shape(HEAD_DIM), flat_shape),
        grid=(B, S // TS),
        in_specs=[pl.BlockSpec((1, TS, D), lambda b, s: (b, s, 0)),
                  pl.BlockSpec((1, 3, D), lambda b, s: (b, 0, 0))]
                 + [full(w) for w in weights] + [full(b_f), full(tri), full(perm)],
        out_specs=(head_spec(KPAD), krow_spec, head_spec(VROWS), flat_spec,
                   head_spec(KPAD), krow_spec, head_spec(HEAD_DIM), flat_spec),
        scratch_shapes=[pltpu.VMEM((N_HEADS, 128), F32)],
        compiler_params=pltpu.CompilerParams(
            dimension_semantics=("arbitrary", "arbitrary"),
            vmem_limit_bytes=56 * 1024 * 1024),
        name="proj",
    )(x, mod3, *weights, b_f, tri, perm)


def _suffix_prod_sublanes(t):
    r = lax.broadcasted_iota(jnp.int32, t.shape, 0)
    incl = t
    for d in (1, 2, 4):
        incl = incl * jnp.where(r + d < STRANDS, pltpu.roll(incl, STRANDS - d, axis=0), 1.0)
    excl = jnp.where(r + 1 < STRANDS, pltpu.roll(incl, STRANDS - 1, axis=0), 1.0)
    return excl, incl


def _sb_weights(z, keep_ref, carry, diagonal):
    keep = 0.5 - 0.5 * jnp.tanh(z)
    if diagonal:
        row = lax.broadcasted_iota(jnp.int32, (BK, BQ), 0)
        sub_row = row % SUB
        kpos = (row // SUB) * SUB + (sub_row % STRANDS) * STRAND_LEN + sub_row // STRANDS
        strict = kpos < lax.broadcasted_iota(jnp.int32, (BK, BQ), 1)
        keep = jnp.where(strict, keep, 1.0)
    keep_ref[...] = keep
    n_sub = BK // SUB
    rows = lambda sub, i: keep_ref[pl.ds(sub * SUB + i * STRANDS, STRANDS), :]
    ws = [[None] * STRAND_LEN for _ in range(n_sub)]
    for sub in reversed(range(n_sub)):
        total = rows(sub, 0)
        for i in range(1, STRAND_LEN):
            total = total * rows(sub, i)
        excl, incl = _suffix_prod_sublanes(total)
        run = carry * excl
        carry = carry * jnp.broadcast_to(incl[0:1], carry.shape)
        for i in reversed(range(STRAND_LEN)):
            nxt = run * rows(sub, i)
            ws[sub][i] = run - nxt
            run = nxt
    w = jnp.concatenate([x for sub_ws in ws for x in sub_ws], axis=0)
    return w.astype(BF16), carry


class _Item:
    def __init__(self, scores, weights, values):
        self.scores, self.weights, self.values = scores, weights, values


def _pipeline(items, finish_pending, prefetch, prefetch_step):
    n = len(items)
    items[0].scores()
    for step in range(1, n + 2):
        if step <= n:
            items[step - 1].weights()
        if step == 1:
            finish_pending()
        if step >= 2:
            items[step - 2].values()
        if step < n:
            items[step].scores()
        if step == prefetch_step:
            prefetch()


def _attn_tiles(refs, qs, tiles, ks_next, ks_pend, state, diagonal):
    qf_ref, kf_ref, vf_ref, qs_ref, ks_ref, vs_ref, s_ref, p_ref, keep_ref = refs
    cur = [None] * (4 * HP) if diagonal else list(state)
    fox_m = lambda h: 2 * h
    fox_acc = lambda h: 2 * h + 1
    sb_carry = lambda h: 2 * HP + 2 * h
    sb_acc = lambda h: 2 * HP + 2 * h + 1

    def finish_pending():
        if not diagonal:
            i = fox_acc(HP - 1)
            cur[i] = cur[i] + jnp.dot(vf_ref[0, HP - 1, :, pl.ds(ks_pend, BK)], p_ref[...],
                                      preferred_element_type=F32)

    def prefetch():
        s_ref[...] = jnp.dot(ks_ref[0, 0, pl.ds(ks_next, BK), :], qs_ref[0, 0, :, pl.ds(qs, BQ)],
                             preferred_element_type=F32)

    def fox_item(h, ti):
        ks = tiles[ti]
        deferred = h == HP - 1 and ti == len(tiles) - 1
        box = {}

        def scores():
            s = jnp.dot(kf_ref[0, h, pl.ds(ks, BK), :], qf_ref[0, h, :, pl.ds(qs, BQ)],
                        preferred_element_type=F32)
            if diagonal:
                kpos = lax.broadcasted_iota(jnp.int32, (BK, BQ), 0)
                qpos = lax.broadcasted_iota(jnp.int32, (BK, BQ), 1)
                s = jnp.where(kpos <= qpos, s, -jnp.inf)
            box["s"] = s

        def weights():
            if diagonal:
                cur[fox_m(h)] = jnp.max(box["s"], axis=0, keepdims=True)
            box["p"] = jnp.exp2(box["s"] - cur[fox_m(h)]).astype(BF16)

        def values():
            if deferred:
                p_ref[...] = box["p"]
                if diagonal:
                    cur[fox_acc(h)] = jnp.zeros((VROWS, BQ), F32)
                return
            pv = jnp.dot(vf_ref[0, h, :, pl.ds(ks, BK)], box["p"], preferred_element_type=F32)
            cur[fox_acc(h)] = pv if diagonal else cur[fox_acc(h)] + pv

        return _Item(scores, weights, values)

    def sb_item(h, ti):
        ks = tiles[ti]
        prefetched = h == 0 and ti == 0 and not diagonal
        box = {}

        def scores():
            if prefetched:
                box["z"] = s_ref[...]
            else:
                box["z"] = jnp.dot(ks_ref[0, h, pl.ds(ks, BK), :], qs_ref[0, h, :, pl.ds(qs, BQ)],
                                   preferred_element_type=F32)

        def weights():
            carry = jnp.ones((STRANDS, BQ), F32) if diagonal else cur[sb_carry(h)]
            box["w"], cur[sb_carry(h)] = _sb_weights(
                box["z"], keep_ref.at[(ti * HP + h) % 2], carry, diagonal)

        def values():
            pv = jnp.dot(vs_ref[0, h, :, pl.ds(ks, BK)], box["w"], preferred_element_type=F32)
            cur[sb_acc(h)] = pv if diagonal else cur[sb_acc(h)] + pv

        return _Item(scores, weights, values)

    items = []
    for ti in range(len(tiles)):
        for h in range(HP):
            items += [sb_item(h, ti), fox_item(h, ti)]
    _pipeline(items, finish_pending, prefetch, max(2, len(items) - 4))
    return tuple(cur)


def _fox_online(qf_ref, kf_ref, vf_ref, h, qs, qi):
    q = qf_ref[0, h, :, pl.ds(qs, BQ)]
    s = jnp.dot(kf_ref[0, h, pl.ds(qs, BK), :], q, preferred_element_type=F32)
    kpos = lax.broadcasted_iota(jnp.int32, (BK, BQ), 0)
    qpos = lax.broadcasted_iota(jnp.int32, (BK, BQ), 1)
    s = jnp.where(kpos <= qpos, s, -jnp.inf)
    m = jnp.max(s, axis=0, keepdims=True)
    acc = jnp.dot(vf_ref[0, h, :, pl.ds(qs, BK)], jnp.exp2(s - m).astype(BF16),
                  preferred_element_type=F32)

    def body(kj, mc):
        m, acc = mc
        ks = pl.multiple_of(kj * BK, BK)
        s = jnp.dot(kf_ref[0, h, pl.ds(ks, BK), :], q, preferred_element_type=F32)
        m_new = jnp.maximum(m, jnp.max(s, axis=0, keepdims=True))
        pv = jnp.dot(vf_ref[0, h, :, pl.ds(ks, BK)], jnp.exp2(s - m_new).astype(BF16),
                     preferred_element_type=F32)
        return m_new, jnp.exp2(m - m_new) * acc + pv

    return lax.fori_loop(0, qi, body, (m, acc))[1]


def _attn_kernel(qf_ref, kf_ref, vf_ref, qs_ref, ks_ref, vs_ref, of_ref, os_ref,
                 s_ref, p_ref, keep_ref, *, seq):
    refs = (qf_ref, kf_ref, vf_ref, qs_ref, ks_ref, vs_ref, s_ref, p_ref, keep_ref)

    def qbody(qi, carry):
        qs = pl.multiple_of(qi * BQ, BQ)
        tile = lambda j: pl.multiple_of(jnp.maximum(j, 0) * BK, BK)
        state = _attn_tiles(refs, qs, [qs], tile(qi - 1), None, None, True)

        def run(n, top, carry):
            st, ks_pend = carry
            tiles = [tile(top - i) for i in range(n)]
            return _attn_tiles(refs, qs, tiles, tile(top - n), pl.multiple_of(ks_pend, BK),
                               st, False), tiles[-1]

        carry_k, done, n = (state, qs), 0, 1
        while n < TILES_PER_BODY:
            chunk = qi & n
            carry_k = lax.cond(chunk != 0, functools.partial(run, n, qi - 1 - done),
                               lambda c: c, carry_k)
            done, n = done + chunk, 2 * n
        bodies = (qi - done) // TILES_PER_BODY

        def kbody(t, carry):
            return run(TILES_PER_BODY, qi - 1 - done - TILES_PER_BODY * t, carry)

        st, ks_pend = lax.fori_loop(0, bodies, kbody, carry_k)

        for h in range(HP):
            os_ref[0, h * HEAD_DIM:(h + 1) * HEAD_DIM, pl.ds(qs, BQ)] = (
                st[2 * HP + 2 * h + 1].astype(os_ref.dtype))

        overflow = jnp.zeros((1, BQ), F32)
        for h in range(HP):
            acc = st[2 * h + 1]
            if h == HP - 1:
                acc = acc + jnp.dot(vf_ref[0, h, :, pl.ds(pl.multiple_of(ks_pend, BK), BK)],
                                    p_ref[...], preferred_element_type=F32)
            bad = jnp.where(jnp.abs(acc) < jnp.inf, 0.0, 1.0)
            overflow = jnp.maximum(overflow, jnp.max(bad, axis=0, keepdims=True))
            of_ref[0, h * HEAD_DIM:(h + 1) * HEAD_DIM, pl.ds(qs, BQ)] = (
                acc[0:HEAD_DIM] / acc[HEAD_DIM:HEAD_DIM + 1]).astype(of_ref.dtype)

        @pl.when(jnp.max(overflow) > 0.0)
        def _():
            for h in range(HP):
                acc = _fox_online(qf_ref, kf_ref, vf_ref, h, qs, qi)
                of_ref[0, h * HEAD_DIM:(h + 1) * HEAD_DIM, pl.ds(qs, BQ)] = (
                    acc[0:HEAD_DIM] / acc[HEAD_DIM:HEAD_DIM + 1]).astype(of_ref.dtype).astype(of_ref.dtype)

        return carry

    lax.fori_loop(0, seq // BQ, qbody, 0)


def _attention(qf, kf, vf, qs, ks, vs):
    B, H, _, S = qf.shape
    once = pl.Buffered(1)
    qspec = pl.BlockSpec((1, HP, KPAD, S), lambda b, g: (b, g, 0, 0), pipeline_mode=once)
    kspec = pl.BlockSpec((1, HP, S, KPAD), lambda b, g: (b, g, 0, 0), pipeline_mode=once)
    vspec = lambda r: pl.BlockSpec((1, HP, r, S), lambda b, g: (b, g, 0, 0), pipeline_mode=once)
    ospec = pl.BlockSpec((1, HP * HEAD_DIM, S), lambda b, g: (b, g, 0))
    oshape = jax.ShapeDtypeStruct((B, H * HEAD_DIM, S), BF16)
    return pl.pallas_call(
        functools.partial(_attn_kernel, seq=S),
        out_shape=(oshape, oshape),
        grid=(B, H // HP),
        in_specs=[qspec, kspec, vspec(VROWS), qspec, kspec, vspec(HEAD_DIM)],
        out_specs=(ospec, ospec),
        scratch_shapes=[pltpu.VMEM((BK, BQ), F32), pltpu.VMEM((BK, BQ), BF16),
                        pltpu.VMEM((HP, BK, BQ), F32)],
        compiler_params=pltpu.CompilerParams(
            dimension_semantics=("arbitrary", "arbitrary"),
            vmem_limit_bytes=56 * 1024 * 1024),
        name="attn",
    )(qf, kf, vf, qs, ks, vs)


def _out_kernel(yf_ref, ys_ref, gf_ref, gs_ref, x_ref, mod_ref, w_ref, lng_ref, lnb_ref, o_ref):
    gated = lambda y_ref, g_ref: y_ref[0].astype(F32) * g_ref[0].astype(F32)
    yg = jnp.concatenate([gated(yf_ref, gf_ref), gated(ys_ref, gs_ref)], axis=0)
    y = jnp.dot(yg.T.astype(BF16), w_ref[...], preferred_element_type=F32)
    gate = mod_ref[0, 2:3, :]
    resid = DEEPNORM_ALPHA * x_ref[0] + gate * y
    mu = jnp.mean(resid, axis=-1, keepdims=True)
    rc = resid - mu
    var = jnp.mean(rc * rc, axis=-1, keepdims=True)
    o_ref[0] = rc * lax.rsqrt(var + LN_EPS) * lng_ref[...] + lnb_ref[...]


def _out(yf, ys, gf, gs, x, mod3, w_out, ln_g, ln_b):
    B, S, D = x.shape
    flat_spec = pl.BlockSpec((1, WIDTH, TS), lambda b, s: (b, 0, s))
    full = lambda a: pl.BlockSpec(a.shape, lambda b, s: (0,) * a.ndim)
    return pl.pallas_call(
        _out_kernel,
        out_shape=jax.ShapeDtypeStruct((B, S, D), F32),
        grid=(B, S // TS),
        in_specs=[flat_spec, flat_spec, flat_spec, flat_spec,
                  pl.BlockSpec((1, TS, D), lambda b, s: (b, s, 0)),
                  pl.BlockSpec((1, 3, D), lambda b, s: (b, 0, 0)),
                  full(w_out), full(ln_g), full(ln_b)],
        out_specs=pl.BlockSpec((1, TS, D), lambda b, s: (b, s, 0)),
        compiler_params=pltpu.CompilerParams(
            dimension_semantics=("arbitrary", "arbitrary"),
            vmem_limit_bytes=48 * 1024 * 1024),
        name="out",
    )(yf, ys, gf, gs, x, mod3, w_out, ln_g, ln_b)


def kernel(x, c, w_ada, b_ada, w_in, b_f, w_out, ln_g, ln_b):
    B, S, D = x.shape
    assert D == D_MODEL and S % TS == 0 and w_ada.shape[0] == DEPTH
    tri = (jnp.arange(TS)[:, None] <= jnp.arange(TS)[None, :]).astype(BF16)
    p = jnp.arange(TS)
    src = (p // SUB) * SUB + (p % STRANDS) * STRAND_LEN + (p % SUB) // STRANDS
    perm = (src[:, None] == jnp.arange(TS)[None, :]).astype(BF16)
    for layer in range(DEPTH):
        c8 = jnp.pad(c, ((0, 8 - B), (0, 0)))
        mod = _ada(c8, w_ada[layer], b_ada[layer][None, :])[:B]
        mod3 = mod.reshape(B, 3, D)
        wt = w_in[layer].T.astype(BF16)
        o = 0
        parts = []
        for wdt in (WIDTH, WIDTH, WIDTH, WIDTH, N_HEADS, WIDTH, WIDTH, WIDTH, WIDTH):
            parts.append(wt[o:o + wdt])
            o += wdt
        wfq, wfk, wfv, wfg, wff, wsq, wsk, wsv, wsg = parts
        wff = jnp.pad(wff, ((0, 16 - N_HEADS), (0, 0)))
        weights = (wfq, wfk, wfv, wfg, wff, wsq, wsk, wsv, wsg)
        qf, kf, vf, gf, qs, ks, vs, gs = _proj(x, mod3, weights, b_f[layer][:, None], tri, perm)
        yf, ys = _attention(qf, kf, vf, qs, ks, vs)
        x = _out(yf, ys, gf, gs, x, mod3, w_out[layer].astype(BF16),
                 ln_g[layer][None, :], ln_b[layer][None, :])
    return x
```

```python
import functools

import jax
import jax.numpy as jnp
from jax import lax
from jax.experimental import pallas as pl
from jax.experimental.pallas import tpu as pltpu

D_MODEL = 1024
HEAD_DIM = 64
N_HEADS = 8
WIDTH = N_HEADS * HEAD_DIM
LN_EPS = 1e-5
DEPTH = 1
DEEPNORM_ALPHA = (2.0 * DEPTH) ** 0.25

TS = 512
BQ = 512
BK = 512
HP = 2
TILES_PER_BODY = 8
SUB = 256
STRANDS = 8
STRAND_LEN = SUB // STRANDS
KPAD = 128
VROWS = 80
LOG2E = 1.4426950408889634

F32 = jnp.float32
BF16 = jnp.bfloat16
NT_DIMS = (((1,), (1,)), ((), ()))


def _split3(f):
    hi = f.astype(BF16)
    r1 = f - hi.astype(F32)
    mid = r1.astype(BF16)
    lo = (r1 - mid.astype(F32)).astype(BF16)
    return hi, mid, lo


def _ada_kernel(c_ref, w_ref, b_ref, o_ref):
    c = c_ref[...]
    a = c * jax.nn.sigmoid(c)
    o_ref[...] = jnp.dot(a, w_ref[...], preferred_element_type=F32,
                         precision=lax.Precision.HIGHEST) + b_ref[...]


def _ada(c8, w_ada, b_ada):
    n = w_ada.shape[1]
    return pl.pallas_call(
        _ada_kernel,
        out_shape=jax.ShapeDtypeStruct((8, n), F32),
        grid=(n // D_MODEL,),
        in_specs=[pl.BlockSpec((8, D_MODEL), lambda j: (0, 0)),
                  pl.BlockSpec((D_MODEL, D_MODEL), lambda j: (0, j)),
                  pl.BlockSpec((1, D_MODEL), lambda j: (0, j))],
        out_specs=pl.BlockSpec((8, D_MODEL), lambda j: (0, j)),
        name="ada",
    )(c8, w_ada, b_ada)


def _proj_kernel(x_ref, mod_ref, wfq_ref, wfk_ref, wfv_ref, wfg_ref, wff_ref,
                 wsq_ref, wsk_ref, wsv_ref, wsg_ref, bf_ref, tri_ref, perm_ref,
                 qf_ref, kf_ref, vf_ref, gf_ref, qs_ref, ks_ref, vs_ref, gs_ref,
                 carry_ref):
    @pl.when(pl.program_id(1) == 0)
    def _():
        carry_ref[...] = jnp.zeros_like(carry_ref)

    x = x_ref[0]
    mu = jnp.mean(x, axis=-1, keepdims=True)
    xc = x - mu
    var = jnp.mean(xc * xc, axis=-1, keepdims=True)
    shift = mod_ref[0, 0:1, :]
    scale = mod_ref[0, 1:2, :]
    u = xc * lax.rsqrt(var + LN_EPS) * (1.0 + scale) + shift
    ub = u.astype(BF16)
    up = jnp.dot(perm_ref[...], ub, preferred_element_type=F32).astype(BF16)

    def proj(w_ref, lhs):
        return lax.dot_general(w_ref[...], lhs, NT_DIMS, preferred_element_type=F32)

    ff = proj(wff_ref, ub)[0:N_HEADS]
    logf = jax.nn.log_sigmoid(ff + bf_ref[...])
    local = jnp.zeros((N_HEADS, TS), F32)
    for part in _split3(logf):
        local = local + jnp.dot(part, tri_ref[...], preferred_element_type=F32)
    fcum = local + carry_ref[:, 0:1]
    carry_ref[...] = jnp.broadcast_to(fcum[:, TS - 1:TS], carry_ref.shape)
    nsplit = [t.astype(F32) for t in _split3(-LOG2E * fcum)]

    zeros_pad = jnp.zeros((KPAD - HEAD_DIM - 16, TS), BF16)
    zeros_half = jnp.zeros((KPAD - HEAD_DIM, TS), BF16)
    ones3 = (lax.broadcasted_iota(jnp.int32, (16, TS), 0) < 3).astype(BF16)
    ones1 = (lax.broadcasted_iota(jnp.int32, (16, TS), 0) < 1).astype(BF16)

    qf = (proj(wfq_ref, ub) * (LOG2E * HEAD_DIM ** -0.5)).astype(BF16)
    kf = proj(wfk_ref, ub)
    vf = proj(wfv_ref, ub).astype(BF16)
    g = proj(wfg_ref, ub)
    gf_ref[0] = (g * jax.nn.sigmoid(g)).astype(gf_ref.dtype)
    for h in range(N_HEADS):
        rows = slice(h * HEAD_DIM, (h + 1) * HEAD_DIM)
        qf_ref[0, h, 0:HEAD_DIM, :] = qf[rows]
        qf_ref[0, h, HEAD_DIM:HEAD_DIM + 16, :] = ones3
        qf_ref[0, h, HEAD_DIM + 16:KPAD, :] = zeros_pad
        kaug = jnp.concatenate(
            [kf[rows]] + [t[h:h + 1] for t in nsplit]
            + [jnp.zeros((KPAD - HEAD_DIM - 3, TS), F32)], axis=0)
        kf_ref[0, h] = kaug.T.astype(BF16)
        vf_ref[0, h, 0:HEAD_DIM, :] = vf[rows]
        vf_ref[0, h, HEAD_DIM:VROWS, :] = ones1

    qs = (proj(wsq_ref, ub) * (0.5 * HEAD_DIM ** -0.5)).astype(BF16)
    ks = proj(wsk_ref, up)
    vs = proj(wsv_ref, up).astype(BF16)
    g = proj(wsg_ref, ub)
    gs_ref[0] = (g * jax.nn.sigmoid(g)).astype(gs_ref.dtype)
    for h in range(N_HEADS):
        rows = slice(h * HEAD_DIM, (h + 1) * HEAD_DIM)
        qs_ref[0, h, 0:HEAD_DIM, :] = qs[rows]
        qs_ref[0, h, HEAD_DIM:KPAD, :] = zeros_half
        kpad = jnp.concatenate([ks[rows], jnp.zeros((KPAD - HEAD_DIM, TS), F32)], axis=0)
        ks_ref[0, h] = kpad.T.astype(BF16)
        vs_ref[0, h] = vs[rows]


def _proj(x, mod3, weights, b_f, tri, perm):
    B, S, D = x.shape
    full = lambda a: pl.BlockSpec(a.shape, lambda b, s: (0,) * a.ndim)
    head_spec = lambda r: pl.BlockSpec((1, N_HEADS, r, TS), lambda b, s: (b, 0, 0, s))
    flat_spec = pl.BlockSpec((1, WIDTH, TS), lambda b, s: (b, 0, s))
    head_shape = lambda r: jax.ShapeDtypeStruct((B, N_HEADS, r, S), BF16)
    flat_shape = jax.ShapeDtypeStruct((B, WIDTH, S), BF16)
    krow_spec = pl.BlockSpec((1, N_HEADS, TS, KPAD), lambda b, s: (b, 0, s, 0))
    krow_shape = jax.ShapeDtypeStruct((B, N_HEADS, S, KPAD), BF16)
    return pl.pallas_call(
        _proj_kernel,
        out_shape=(head_shape(KPAD), krow_shape, head_shape(VROWS), flat_shape,
                   head_shape(KPAD), krow_shape, head_shape(HEAD_DIM), flat_shape),
        grid=(B, S // TS),
        in_specs=[pl.BlockSpec((1, TS, D), lambda b, s: (b, s, 0)),
                  pl.BlockSpec((1, 3, D), lambda b, s: (b, 0, 0))]
                 + [full(w) for w in weights] + [full(b_f), full(tri), full(perm)],
        out_specs=(head_spec(KPAD), krow_spec, head_spec(VROWS), flat_spec,
                   head_spec(KPAD), krow_spec, head_spec(HEAD_DIM), flat_spec),
        scratch_shapes=[pltpu.VMEM((N_HEADS, 128), F32)],
        compiler_params=pltpu.CompilerParams(
            dimension_semantics=("arbitrary", "arbitrary"),
            vmem_limit_bytes=56 * 1024 * 1024),
        name="proj",
    )(x, mod3, *weights, b_f, tri, perm)


def _suffix_prod_sublanes(t):
    r = lax.broadcasted_iota(jnp.int32, t.shape, 0)
    incl = t
    for d in (1, 2, 4):
        incl = incl * jnp.where(r + d < STRANDS, pltpu.roll(incl, STRANDS - d, axis=0), 1.0)
    excl = jnp.where(r + 1 < STRANDS, pltpu.roll(incl, STRANDS - 1, axis=0), 1.0)
    return excl, incl


def _sb_weights(z, keep_ref, carry, diagonal):
    keep = 0.5 - 0.5 * jnp.tanh(z)
    if diagonal:
        row = lax.broadcasted_iota(jnp.int32, (BK, BQ), 0)
        sub_row = row % SUB
        kpos = (row // SUB) * SUB + (sub_row % STRANDS) * STRAND_LEN + sub_row // STRANDS
        strict = kpos < lax.broadcasted_iota(jnp.int32, (BK, BQ), 1)
        keep = jnp.where(strict, keep, 1.0)
    keep_ref[...] = keep
    n_sub = BK // SUB
    rows = lambda sub, i: keep_ref[pl.ds(sub * SUB + i * STRANDS, STRANDS), :]
    ws = [[None] * STRAND_LEN for _ in range(n_sub)]
    for sub in reversed(range(n_sub)):
        total = rows(sub, 0)
        for i in range(1, STRAND_LEN):
            total = total * rows(sub, i)
        excl, incl = _suffix_prod_sublanes(total)
        run = carry * excl
        carry = carry * jnp.broadcast_to(incl[0:1], carry.shape)
        for i in reversed(range(STRAND_LEN)):
            nxt = run * rows(sub, i)
            ws[sub][i] = run - nxt
            run = nxt
    w = jnp.concatenate([x for sub_ws in ws for x in sub_ws], axis=0)
    return w.astype(BF16), carry


class _Item:
    def __init__(self, scores, weights, values):
        self.scores, self.weights, self.values = scores, weights, values


def _pipeline(items, finish_pending, prefetch, prefetch_step):
    n = len(items)
    items[0].scores()
    for step in range(1, n + 2):
        if step <= n:
            items[step - 1].weights()
        if step == 1:
            finish_pending()
        if step >= 2:
            items[step - 2].values()
        if step < n:
            items[step].scores()
        if step == prefetch_step:
            prefetch()


def _attn_tiles(refs, qs, tiles, ks_next, ks_pend, state, diagonal):
    qf_ref, kf_ref, vf_ref, qs_ref, ks_ref, vs_ref, s_ref, p_ref, keep_ref = refs
    cur = [None] * (4 * HP) if diagonal else list(state)
    fox_m = lambda h: 2 * h
    fox_acc = lambda h: 2 * h + 1
    sb_carry = lambda h: 2 * HP + 2 * h
    sb_acc = lambda h: 2 * HP + 2 * h + 1

    def finish_pending():
        if not diagonal:
            i = fox_acc(HP - 1)
            cur[i] = cur[i] + jnp.dot(vf_ref[0, HP - 1, :, pl.ds(ks_pend, BK)], p_ref[...],
                                      preferred_element_type=F32)

    def prefetch():
        s_ref[...] = jnp.dot(ks_ref[0, 0, pl.ds(ks_next, BK), :], qs_ref[0, 0, :, pl.ds(qs, BQ)],
                             preferred_element_type=F32)

    def fox_item(h, ti):
        ks = tiles[ti]
        deferred = h == HP - 1 and ti == len(tiles) - 1
        box = {}

        def scores():
            s = jnp.dot(kf_ref[0, h, pl.ds(ks, BK), :], qf_ref[0, h, :, pl.ds(qs, BQ)],
                        preferred_element_type=F32)
            if diagonal:
                kpos = lax.broadcasted_iota(jnp.int32, (BK, BQ), 0)
                qpos = lax.broadcasted_iota(jnp.int32, (BK, BQ), 1)
                s = jnp.where(kpos <= qpos, s, -jnp.inf)
            box["s"] = s

        def weights():
            if diagonal:
                cur[fox_m(h)] = jnp.max(box["s"], axis=0, keepdims=True)
            box["p"] = jnp.exp2(box["s"] - cur[fox_m(h)]).astype(BF16)

        def values():
            if deferred:
                p_ref[...] = box["p"]
                if diagonal:
                    cur[fox_acc(h)] = jnp.zeros((VROWS, BQ), F32)
                return
            pv = jnp.dot(vf_ref[0, h, :, pl.ds(ks, BK)], box["p"], preferred_element_type=F32)
            cur[fox_acc(h)] = pv if diagonal else cur[fox_acc(h)] + pv

        return _Item(scores, weights, values)

    def sb_item(h, ti):
        ks = tiles[ti]
        prefetched = h == 0 and ti == 0 and not diagonal
        box = {}

        def scores():
            if prefetched:
                box["z"] = s_ref[...]
            else:
                box["z"] = jnp.dot(ks_ref[0, h, pl.ds(ks, BK), :], qs_ref[0, h, :, pl.ds(qs, BQ)],
                                   preferred_element_type=F32)

        def weights():
            carry = jnp.ones((STRANDS, BQ), F32) if diagonal else cur[sb_carry(h)]
            box["w"], cur[sb_carry(h)] = _sb_weights(
                box["z"], keep_ref.at[(ti * HP + h) % 2], carry, diagonal)

        def values():
            pv = jnp.dot(vs_ref[0, h, :, pl.ds(ks, BK)], box["w"], preferred_element_type=F32)
            cur[sb_acc(h)] = pv if diagonal else cur[sb_acc(h)] + pv

        return _Item(scores, weights, values)

    items = []
    for ti in range(len(tiles)):
        for h in range(HP):
            items += [sb_item(h, ti), fox_item(h, ti)]
    _pipeline(items, finish_pending, prefetch, max(2, len(items) - 4))
    return tuple(cur)


def _fox_online(qf_ref, kf_ref, vf_ref, h, qs, qi):
    q = qf_ref[0, h, :, pl.ds(qs, BQ)]
    s = jnp.dot(kf_ref[0, h, pl.ds(qs, BK), :], q, preferred_element_type=F32)
    kpos = lax.broadcasted_iota(jnp.int32, (BK, BQ), 0)
    qpos = lax.broadcasted_iota(jnp.int32, (BK, BQ), 1)
    s = jnp.where(kpos <= qpos, s, -jnp.inf)
    m = jnp.max(s, axis=0, keepdims=True)
    acc = jnp.dot(vf_ref[0, h, :, pl.ds(qs, BK)], jnp.exp2(s - m).astype(BF16),
                  preferred_element_type=F32)

    def body(kj, mc):
        m, acc = mc
        ks = pl.multiple_of(kj * BK, BK)
        s = jnp.dot(kf_ref[0, h, pl.ds(ks, BK), :], q, preferred_element_type=F32)
        m_new = jnp.maximum(m, jnp.max(s, axis=0, keepdims=True))
        pv = jnp.dot(vf_ref[0, h, :, pl.ds(ks, BK)], jnp.exp2(s - m_new).astype(BF16),
                     preferred_element_type=F32)
        return m_new, jnp.exp2(m - m_new) * acc + pv

    return lax.fori_loop(0, qi, body, (m, acc))[1]


def _attn_kernel(qf_ref, kf_ref, vf_ref, qs_ref, ks_ref, vs_ref, of_ref, os_ref,
                 s_ref, p_ref, keep_ref, *, seq):
    refs = (qf_ref, kf_ref, vf_ref, qs_ref, ks_ref, vs_ref, s_ref, p_ref, keep_ref)

    def qbody(qi, carry):
        qs = pl.multiple_of(qi * BQ, BQ)
        tile = lambda j: pl.multiple_of(jnp.maximum(j, 0) * BK, BK)
        state = _attn_tiles(refs, qs, [qs], tile(qi - 1), None, None, True)

        def run(n, top, carry):
            st, ks_pend = carry
            tiles = [tile(top - i) for i in range(n)]
            return _attn_tiles(refs, qs, tiles, tile(top - n), pl.multiple_of(ks_pend, BK),
                               st, False), tiles[-1]

        carry_k, done, n = (state, qs), 0, 1
        while n < TILES_PER_BODY:
            chunk = qi & n
            carry_k = lax.cond(chunk != 0, functools.partial(run, n, qi - 1 - done),
                               lambda c: c, carry_k)
            done, n = done + chunk, 2 * n
        bodies = (qi - done) // TILES_PER_BODY

        def kbody(t, carry):
            return run(TILES_PER_BODY, qi - 1 - done - TILES_PER_BODY * t, carry)

        st, ks_pend = lax.fori_loop(0, bodies, kbody, carry_k)

        for h in range(HP):
            os_ref[0, h * HEAD_DIM:(h + 1) * HEAD_DIM, pl.ds(qs, BQ)] = (
                st[2 * HP + 2 * h + 1].astype(os_ref.dtype))

        overflow = jnp.zeros((1, BQ), F32)
        for h in range(HP):
            acc = st[2 * h + 1]
            if h == HP - 1:
                acc = acc + jnp.dot(vf_ref[0, h, :, pl.ds(pl.multiple_of(ks_pend, BK), BK)],
                                    p_ref[...], preferred_element_type=F32)
            bad = jnp.where(jnp.abs(acc) < jnp.inf, 0.0, 1.0)
            overflow = jnp.maximum(overflow, jnp.max(bad, axis=0, keepdims=True))
            of_ref[0, h * HEAD_DIM:(h + 1) * HEAD_DIM, pl.ds(qs, BQ)] = (
                acc[0:HEAD_DIM] / acc[HEAD_DIM:HEAD_DIM + 1]).astype(of_ref.dtype)

        @pl.when(jnp.max(overflow) > 0.0)
        def _():
            for h in range(HP):
                acc = _fox_online(qf_ref, kf_ref, vf_ref, h, qs, qi)
                of_ref[0, h * HEAD_DIM:(h + 1) * HEAD_DIM, pl.ds(qs, BQ)] = (
                    acc[0:HEAD_DIM] / acc[HEAD_DIM:HEAD_DIM + 1]).astype(of_ref.dtype).astype(of_ref.dtype)

        return carry

    lax.fori_loop(0, seq // BQ, qbody, 0)


def _attention(qf, kf, vf, qs, ks, vs):
    B, H, _, S = qf.shape
    once = pl.Buffered(1)
    qspec = pl.BlockSpec((1, HP, KPAD, S), lambda b, g: (b, g, 0, 0), pipeline_mode=once)
    kspec = pl.BlockSpec((1, HP, S, KPAD), lambda b, g: (b, g, 0, 0), pipeline_mode=once)
    vspec = lambda r: pl.BlockSpec((1, HP, r, S), lambda b, g: (b, g, 0, 0), pipeline_mode=once)
    ospec = pl.BlockSpec((1, HP * HEAD_DIM, S), lambda b, g: (b, g, 0))
    oshape = jax.ShapeDtypeStruct((B, H * HEAD_DIM, S), BF16)
    return pl.pallas_call(
        functools.partial(_attn_kernel, seq=S),
        out_shape=(oshape, oshape),
        grid=(B, H // HP),
        in_specs=[qspec, kspec, vspec(VROWS), qspec, kspec, vspec(HEAD_DIM)],
        out_specs=(ospec, ospec),
        scratch_shapes=[pltpu.VMEM((BK, BQ), F32), pltpu.VMEM((BK, BQ), BF16),
                        pltpu.VMEM((HP, BK, BQ), F32)],
        compiler_params=pltpu.CompilerParams(
            dimension_semantics=("arbitrary", "arbitrary"),
            vmem_limit_bytes=56 * 1024 * 1024),
        name="attn",
    )(qf, kf, vf, qs, ks, vs)


def _out_kernel(yf_ref, ys_ref, gf_ref, gs_ref, x_ref, mod_ref, w_ref, lng_ref, lnb_ref, o_ref):
    gated = lambda y_ref, g_ref: y_ref[0].astype(F32) * g_ref[0].astype(F32)
    yg = jnp.concatenate([gated(yf_ref, gf_ref), gated(ys_ref, gs_ref)], axis=0)
    y = jnp.dot(yg.T.astype(BF16), w_ref[...], preferred_element_type=F32)
    gate = mod_ref[0, 2:3, :]
    resid = DEEPNORM_ALPHA * x_ref[0] + gate * y
    mu = jnp.mean(resid, axis=-1, keepdims=True)
    rc = resid - mu
    var = jnp.mean(rc * rc, axis=-1, keepdims=True)
    o_ref[0] = rc * lax.rsqrt(var + LN_EPS) * lng_ref[...] + lnb_ref[...]


def _out(yf, ys, gf, gs, x, mod3, w_out, ln_g, ln_b):
    B, S, D = x.shape
    flat_spec = pl.BlockSpec((1, WIDTH, TS), lambda b, s: (b, 0, s))
    full = lambda a: pl.BlockSpec(a.shape, lambda b, s: (0,) * a.ndim)
    return pl.pallas_call(
        _out_kernel,
        out_shape=jax.ShapeDtypeStruct((B, S, D), F32),
        grid=(B, S // TS),
        in_specs=[flat_spec, flat_spec, flat_spec, flat_spec,
                  pl.BlockSpec((1, TS, D), lambda b, s: (b, s, 0)),
                  pl.BlockSpec((1, 3, D), lambda b, s: (b, 0, 0)),
                  full(w_out), full(ln_g), full(ln_b)],
        out_specs=pl.BlockSpec((1, TS, D), lambda b, s: (b, s, 0)),
        compiler_params=pltpu.CompilerParams(
            dimension_semantics=("arbitrary", "arbitrary"),
            vmem_limit_bytes=48 * 1024 * 1024),
        name="out",
    )(yf, ys, gf, gs, x, mod3, w_out, ln_g, ln_b)


def kernel(x, c, w_ada, b_ada, w_in, b_f, w_out, ln_g, ln_b):
    B, S, D = x.shape
    assert D == D_MODEL and S % TS == 0 and w_ada.shape[0] == DEPTH
    tri = (jnp.arange(TS)[:, None] <= jnp.arange(TS)[None, :]).astype(BF16)
    p = jnp.arange(TS)
    src = (p // SUB) * SUB + (p % STRANDS) * STRAND_LEN + (p % SUB) // STRANDS
    perm = (src[:, None] == jnp.arange(TS)[None, :]).astype(BF16)
    for layer in range(DEPTH):
        c8 = jnp.pad(c, ((0, 8 - B), (0, 0)))
        mod = _ada(c8, w_ada[layer], b_ada[layer][None, :])[:B]
        mod3 = mod.reshape(B, 3, D)
        wt = w_in[layer].T.astype(BF16)
        o = 0
        parts = []
        for wdt in (WIDTH, WIDTH, WIDTH, WIDTH, N_HEADS, WIDTH, WIDTH, WIDTH, WIDTH):
            parts.append(wt[o:o + wdt])
            o += wdt
        wfq, wfk, wfv, wfg, wff, wsq, wsk, wsv, wsg = parts
        wff = jnp.pad(wff, ((0, 16 - N_HEADS), (0, 0)))
        weights = (wfq, wfk, wfv, wfg, wff, wsq, wsk, wsv, wsg)
        qf, kf, vf, gf, qs, ks, vs, gs = _proj(x, mod3, weights, b_f[layer][:, None], tri, perm)
        yf, ys = _attention(qf, kf, vf, qs, ks, vs)
        x = _out(yf, ys, gf, gs, x, mod3, w_out[layer].astype(BF16),
                 ln_g[layer][None, :], ln_b[layer][None, :])
    return x
```
